```python
import jax, jax.numpy as jnp
from jax import lax
import numpy as np

D_MODEL = 1024
BATCH = 8
SEQ = 2048
DEPTH = 4
DEC_BATCH = 128
DEC_SEQ = 1
PAST_LEN = 16384
PAGE_SIZE = 128

N_EVEN = (DEPTH + 1) // 2
N_ODD = DEPTH // 2
MIX_A = D_MODEL // 2
MIX_B = D_MODEL - MIX_A
HG_HEAD_DIM = 128
HG_HEADS = MIX_A // HG_HEAD_DIM
GDN_HEAD_DIM = 128
GDN_HEADS = MIX_B // GDN_HEAD_DIM
GDN_CONV = 4
POOL_WINDOWS = (2, 4, 8, 16)
POOL_GROUPS = len(POOL_WINDOWS)
POOL_GROUP_DIM = D_MODEL // POOL_GROUPS
POOL_BUF = max(POOL_WINDOWS) - 1
D_FF = ((8 * D_MODEL // 3 + 127) // 128) * 128
IN_EVEN = 4 * MIX_A + 4 * MIX_B + 2 * GDN_HEADS
HG_CHUNK = 32
GDN_CHUNK = 64
EPS = 1e-6

kernel_name = 'hybrid_hgrn2_gdn_pool_decoder_step'


def rmsnorm(x, w):
    xf = x.astype(jnp.float32)
    y = xf * lax.rsqrt(jnp.mean(xf * xf, axis=-1, keepdims=True) + EPS)
    return (y * w.astype(jnp.float32)).astype(x.dtype)


def l2norm(t):
    return t * lax.rsqrt(jnp.sum(t * t, axis=-1, keepdims=True) + EPS)


def swiglu(h, wg, wu, wd):
    return (jax.nn.silu(h @ wg) * (h @ wu)) @ wd


def _chunks(t, c):
    b, T = t.shape[:2]
    n = -(-T // c)
    t = jnp.pad(t, [(0, 0), (0, n * c - T)] + [(0, 0)] * (t.ndim - 2))
    return jnp.moveaxis(t.reshape((b, n, c) + t.shape[2:]), 1, 0)


def _unchunks(o, T):
    n, b, c = o.shape[:3]
    return jnp.moveaxis(o, 0, 1).reshape((b, n * c) + o.shape[3:])[:, :T]


def hgrn2_recurrence(q, k, v, logg, s0):
    T = q.shape[1]
    c = min(HG_CHUNK, T)
    causal = jnp.tril(jnp.ones((c, c), dtype=bool))

    def step(s, inp):
        qc, kc, vc, gc = inp
        b = jnp.cumsum(gc, axis=1)
        diff = b[:, :, None] - b[:, None, :]
        dec = jnp.exp(jnp.where(causal[None, :, :, None, None], diff, -jnp.inf))
        att = jnp.einsum('bthk,btshk,bshk->bhts', qc, dec, kc)
        o = (jnp.einsum('bhts,bshv->bthv', att, vc)
             + jnp.einsum('bthk,bhkv->bthv', qc * jnp.exp(b), s))
        bl = b[:, -1]
        s = (jnp.exp(bl)[..., None] * s
             + jnp.einsum('bshk,bshv->bhkv', kc * jnp.exp(bl[:, None] - b), vc))
        return s, o

    xs = (_chunks(q, c), _chunks(k, c), _chunks(v, c), _chunks(logg, c))
    s, o = lax.scan(step, s0, xs)
    return _unchunks(o, T), s


def gated_delta_recurrence(q, k, v, g, beta, s0):
    T = q.shape[1]
    c = min(GDN_CHUNK, T)
    dv = v.shape[-1]
    incl = jnp.tril(jnp.ones((c, c), dtype=bool))
    strict = jnp.tril(jnp.ones((c, c), dtype=bool), -1)
    eye = jnp.eye(c, dtype=jnp.float32)

    def step(s, inp):
        qc, kc, vc, gc, bc = inp
        b = jnp.cumsum(gc, axis=1)
        bh = jnp.swapaxes(b, 1, 2)
        diff = bh[..., :, None] - bh[..., None, :]
        L = jnp.exp(jnp.where(incl, diff, -jnp.inf))
        kb = kc * bc[..., None]
        akk = jnp.where(strict, jnp.einsum('bthk,bshk->bhts', kb, kc) * L, 0.0)
        rhs = jnp.concatenate([vc * bc[..., None], kb * jnp.exp(b)[..., None]], axis=-1)
        rhs = jnp.swapaxes(rhs, 1, 2)
        sol = lax.linalg.triangular_solve(eye + akk, rhs, left_side=True, lower=True,
                                          unit_diagonal=True)
        u, w = sol[..., :dv], sol[..., dv:]
        v_new = u - jnp.einsum('bhck,bhkv->bhcv', w, s)
        aqk = jnp.einsum('bthk,bshk->bhts', qc, kc) * L
        o = (jnp.einsum('bthk,bhkv->bhtv', qc * jnp.exp(b)[..., None], s)
             + jnp.einsum('bhts,bhsv->bhtv', aqk, v_new))
        bl = bh[..., -1]
        s = (jnp.exp(bl)[..., None, None] * s
             + jnp.einsum('bhs,bshk,bhsv->bhkv', jnp.exp(bl[..., None] - bh), kc, v_new))
        return s, jnp.swapaxes(o, 1, 2)

    xs = (_chunks(q, c), _chunks(k, c), _chunks(v, c), _chunks(g, c), _chunks(beta, c))
    s, o = lax.scan(step, s0, xs)
    return _unchunks(o, T), s


def even_mixer(h, s_hg, s_gdn, conv_buf, lb, w_in, hg_norm_w, conv_w, a_log, dt_bias,
               gdn_norm_w, w_out):
    f32 = jnp.float32
    B, T, _ = h.shape
    p = h @ w_in
    sizes = (MIX_A, MIX_A, MIX_A, MIX_A, 3 * MIX_B, MIX_B, GDN_HEADS, GDN_HEADS)
    idx = np.cumsum(sizes)[:-1].tolist()
    qa, fa, ia, oga, qkv, z, a, bb = jnp.split(p, idx, axis=-1)

    def heads(t, n):
        return t.reshape(B, T, n, -1)

    lb = lb.astype(f32)
    fa32 = fa.astype(f32)
    logg = jnp.logaddexp(jnp.log(lb), jnp.log1p(-lb) + jax.nn.log_sigmoid(fa32))
    k_a = (1.0 - lb) * jax.nn.sigmoid(-fa32)
    q_a = jax.nn.silu(qa.astype(f32)) * HG_HEAD_DIM ** -0.5
    o_a, s_hg_new = hgrn2_recurrence(heads(q_a, HG_HEADS), heads(k_a, HG_HEADS),
                                     heads(ia.astype(f32), HG_HEADS), heads(logg, HG_HEADS),
                                     s_hg.astype(f32))
    o_a = rmsnorm(o_a, hg_norm_w) * jax.nn.sigmoid(heads(oga.astype(f32), HG_HEADS))

    ext = jnp.concatenate([conv_buf.astype(qkv.dtype), qkv], axis=1)
    conv = lax.conv_general_dilated(ext, conv_w.astype(ext.dtype)[:, None, :], (1,), 'VALID',
                                    dimension_numbers=('NWC', 'WIO', 'NWC'),
                                    feature_group_count=3 * MIX_B)
    conv = jax.nn.silu(conv.astype(f32))
    qb, kb, vb = jnp.split(conv, 3, axis=-1)
    qb = l2norm(heads(qb, GDN_HEADS)) * GDN_HEAD_DIM ** -0.5
    kb = l2norm(heads(kb, GDN_HEADS))
    g = -jnp.exp(a_log.astype(f32)) * jax.nn.softplus(a.astype(f32) + dt_bias.astype(f32))
    beta = jax.nn.sigmoid(bb.astype(f32))
    o_b, s_gdn_new = gated_delta_recurrence(qb, kb, heads(vb, GDN_HEADS), g, beta,
                                            s_gdn.astype(f32))
    o_b = rmsnorm(o_b, gdn_norm_w) * jax.nn.silu(heads(z.astype(f32), GDN_HEADS))

    o = jnp.concatenate([o_a.reshape(B, T, MIX_A), o_b.reshape(B, T, MIX_B)], axis=-1)
    y = o.astype(h.dtype) @ w_out
    return (y, s_hg_new.astype(s_hg.dtype), s_gdn_new.astype(s_gdn.dtype),
            ext[:, -(GDN_CONV - 1):].astype(conv_buf.dtype))


def pool_mixer(h, buf, pos0, w_in, w_grp, scale):
    f32 = jnp.float32
    B, T, _ = h.shape
    u = h @ w_in
    ext = jnp.concatenate([buf.astype(u.dtype), u], axis=1)
    cs = jnp.pad(jnp.cumsum(ext.astype(f32), axis=1), ((0, 0), (1, 0), (0, 0)))
    pos = pos0 + jnp.arange(T)
    o0 = POOL_BUF + 1
    diffs = []
    for gi, w in enumerate(POOL_WINDOWS):
        sl = slice(gi * POOL_GROUP_DIM, (gi + 1) * POOL_GROUP_DIM)
        win = cs[:, o0:o0 + T, sl] - cs[:, o0 - w:o0 - w + T, sl]
        cnt = jnp.minimum(pos + 1, w).astype(f32)[None, :, None]
        diffs.append(win / cnt - u[..., sl].astype(f32))
    d = jnp.stack(diffs, axis=2)
    y = jnp.einsum('btgc,gcd->btgd', d, w_grp.astype(f32)).reshape(B, T, D_MODEL)
    y = y * scale.astype(f32)
    return y.astype(h.dtype), ext[:, -POOL_BUF:].astype(buf.dtype)


def trunk(x, s_hg, s_gdn, s_conv, s_pool, pos0, prm):
    lb_all = jnp.cumsum(jax.nn.softmax(prm['hgrn_lb'].astype(jnp.float32), axis=0), axis=0)
    lb_all = lb_all - lb_all[:1]
    new_hg, new_gdn, new_conv, new_pool = [], [], [], []
    for li in range(DEPTH):
        x = x + 0.5 * swiglu(rmsnorm(x, prm['norm_ffn1'][li]), prm['ffn1_w_gate'][li],
                             prm['ffn1_w_up'][li], prm['ffn1_w_down'][li])
        h = rmsnorm(x, prm['norm_mix'][li])
        if li % 2 == 0:
            e = li // 2
            y, a, b, c = even_mixer(h, s_hg[e], s_gdn[e], s_conv[e], lb_all[e],
                                    prm['w_in_even'][e], prm['hgrn_norm_w'][e],
                                    prm['gdn_conv_w'][e], prm['gdn_a_log'][e],
                                    prm['gdn_dt_bias'][e], prm['gdn_norm_w'][e],
                                    prm['w_out_even'][e])
            new_hg.append(a)
            new_gdn.append(b)
            new_conv.append(c)
        else:
            o = li // 2
            y, pb = pool_mixer(h, s_pool[o], pos0, prm['pool_w_in'][o], prm['pool_w_group'][o],
                               prm['pool_scale'][o])
            new_pool.append(pb)
        x = x + y
        x = x + 0.5 * swiglu(rmsnorm(x, prm['norm_ffn2'][li]), prm['ffn2_w_gate'][li],
                             prm['ffn2_w_up'][li], prm['ffn2_w_down'][li])
    y = rmsnorm(x, prm['final_norm'])
    return y, jnp.stack(new_hg), jnp.stack(new_gdn), jnp.stack(new_conv), jnp.stack(new_pool)


def setup_inputs(seed: int = 0) -> dict:
    key = jax.random.key(seed)
    ks = jax.random.split(key, 32)
    f32 = jnp.float32

    def nrm(i, shape, s):
        return jax.random.normal(ks[i], shape, f32) * s

    dt = jnp.exp(jax.random.uniform(ks[10], (N_EVEN, GDN_HEADS), f32,
                                    np.log(1e-3).astype(np.float32), np.log(1e-1).astype(np.float32)))
    return {
        'x_prompt': nrm(0, (BATCH, SEQ, D_MODEL), 1.0),
        'x_sample': nrm(1, (DEC_BATCH, DEC_SEQ, D_MODEL), 1.0),
        'state_hgrn': nrm(2, (N_EVEN, DEC_BATCH, HG_HEADS, HG_HEAD_DIM, HG_HEAD_DIM), 0.5),
        'state_gdn': nrm(3, (N_EVEN, DEC_BATCH, GDN_HEADS, GDN_HEAD_DIM, GDN_HEAD_DIM), 0.1),
        'state_gdn_conv': nrm(4, (N_EVEN, DEC_BATCH, GDN_CONV - 1, 3 * MIX_B), 1.0),
        'state_pool': nrm(5, (N_ODD, DEC_BATCH, POOL_BUF, D_MODEL), 1.0),
        'hgrn_lb': 1.0 + nrm(6, (N_EVEN, MIX_A), 0.1),
        'w_in_even': nrm(7, (N_EVEN, D_MODEL, IN_EVEN), D_MODEL ** -0.5),
        'hgrn_norm_w': 1.0 + nrm(8, (N_EVEN, HG_HEAD_DIM), 0.05),
        'gdn_conv_w': nrm(9, (N_EVEN, GDN_CONV, 3 * MIX_B), 0.5),
        'gdn_a_log': jnp.log(jax.random.uniform(ks[11], (N_EVEN, GDN_HEADS), f32, 1.0, 16.0)),
        'gdn_dt_bias': dt + jnp.log(-jnp.expm1(-dt)),
        'gdn_norm_w': 1.0 + nrm(12, (N_EVEN, GDN_HEAD_DIM), 0.05),
        'w_out_even': nrm(13, (N_EVEN, D_MODEL, D_MODEL), D_MODEL ** -0.5),
        'pool_w_in': nrm(14, (N_ODD, D_MODEL, D_MODEL), D_MODEL ** -0.5),
        'pool_w_group': nrm(15, (N_ODD, POOL_GROUPS, POOL_GROUP_DIM, POOL_GROUP_DIM),
                            POOL_GROUP_DIM ** -0.5),
        'pool_scale': 1.0 + nrm(16, (N_ODD, D_MODEL), 0.1),
        'norm_ffn1': 1.0 + nrm(17, (DEPTH, D_MODEL), 0.05),
        'ffn1_w_gate': nrm(18, (DEPTH, D_MODEL, D_FF), D_MODEL ** -0.5),
        'ffn1_w_up': nrm(19, (DEPTH, D_MODEL, D_FF), D_MODEL ** -0.5),
        'ffn1_w_down': nrm(20, (DEPTH, D_FF, D_MODEL), D_FF ** -0.5),
        'norm_mix': 1.0 + nrm(21, (DEPTH, D_MODEL), 0.05),
        'norm_ffn2': 1.0 + nrm(22, (DEPTH, D_MODEL), 0.05),
        'ffn2_w_gate': nrm(23, (DEPTH, D_MODEL, D_FF), D_MODEL ** -0.5),
        'ffn2_w_up': nrm(24, (DEPTH, D_MODEL, D_FF), D_MODEL ** -0.5),
        'ffn2_w_down': nrm(25, (DEPTH, D_FF, D_MODEL), D_FF ** -0.5),
        'final_norm': 1.0 + nrm(26, (D_MODEL,), 0.05),
    }


def reference(x_prompt, x_sample, state_hgrn, state_gdn, state_gdn_conv, state_pool,
              hgrn_lb, w_in_even, hgrn_norm_w, gdn_conv_w, gdn_a_log, gdn_dt_bias, gdn_norm_w,
              w_out_even, pool_w_in, pool_w_group, pool_scale, norm_ffn1, ffn1_w_gate, ffn1_w_up,
              ffn1_w_down, norm_mix, norm_ffn2, ffn2_w_gate, ffn2_w_up, ffn2_w_down, final_norm):
    prm = dict(hgrn_lb=hgrn_lb, w_in_even=w_in_even, hgrn_norm_w=hgrn_norm_w,
               gdn_conv_w=gdn_conv_w, gdn_a_log=gdn_a_log, gdn_dt_bias=gdn_dt_bias,
               gdn_norm_w=gdn_norm_w, w_out_even=w_out_even, pool_w_in=pool_w_in,
               pool_w_group=pool_w_group, pool_scale=pool_scale, norm_ffn1=norm_ffn1,
               ffn1_w_gate=ffn1_w_gate, ffn1_w_up=ffn1_w_up, ffn1_w_down=ffn1_w_down,
               norm_mix=norm_mix, norm_ffn2=norm_ffn2, ffn2_w_gate=ffn2_w_gate,
               ffn2_w_up=ffn2_w_up, ffn2_w_down=ffn2_w_down, final_norm=final_norm)
    bp = x_prompt.shape[0]
    z_hg = jnp.zeros((N_EVEN, bp) + state_hgrn.shape[2:], state_hgrn.dtype)
    z_gdn = jnp.zeros((N_EVEN, bp) + state_gdn.shape[2:], state_gdn.dtype)
    z_conv = jnp.zeros((N_EVEN, bp) + state_gdn_conv.shape[2:], state_gdn_conv.dtype)
    z_pool = jnp.zeros((N_ODD, bp) + state_pool.shape[2:], state_pool.dtype)
    y_prompt, hg_p, gdn_p, conv_p, pool_p = trunk(x_prompt, z_hg, z_gdn, z_conv, z_pool, 0, prm)
    y_sample, hg_s, gdn_s, conv_s, pool_s = trunk(x_sample, state_hgrn, state_gdn, state_gdn_conv,
                                                  state_pool, PAST_LEN, prm)
    return (y_prompt, y_sample, hg_p, gdn_p, conv_p, pool_p, hg_s, gdn_s, conv_s, pool_s)
```

```python
import functools

import jax
import jax.numpy as jnp
from jax import lax
from jax.experimental import pallas as pl
from jax.experimental.pallas import tpu as pltpu

F32 = jnp.float32
BF16 = jnp.bfloat16
EPS = 1e-6
HEAD_DIM = 128
N_HEADS = 4
GDN_CHUNK = 64
HG_BAND = 8
POOL_WINDOWS = (2, 4, 8, 16)
POOL_BUF = max(POOL_WINDOWS) - 1
GDN_CONV = 4
VMEM_LIMIT = 56 * 1024 * 1024
NEG_INF = float("-inf")


def _dot(a, b):
    return jnp.dot(a, b, preferred_element_type=F32)


def _dot_nt(a, b):
    return lax.dot_general(a, b, (((1,), (1,)), ((), ())), preferred_element_type=F32)


def _dot_tn(a, b):
    return lax.dot_general(a, b, (((0,), (0,)), ((), ())), preferred_element_type=F32)


def _dot_exact(a, b):
    return jnp.dot(a, b, preferred_element_type=F32, precision=lax.Precision.HIGHEST)


def _rms(x, w):
    return x * lax.rsqrt(jnp.mean(x * x, axis=-1, keepdims=True) + EPS) * w


def _sigmoid(x):
    return 1.0 / (1.0 + jnp.exp(-x))


def _silu(x):
    return x * _sigmoid(x)


def _softplus(x):
    return jnp.maximum(x, 0.0) + jnp.log1p(jnp.exp(-jnp.abs(x)))


def _log_sigmoid(x):
    return jnp.minimum(x, 0.0) - jnp.log1p(jnp.exp(-jnp.abs(x)))


def _logaddexp(a, b):
    return jnp.maximum(a, b) + jnp.log1p(jnp.exp(-jnp.abs(a - b)))


def _head_rms(o, w):
    return o * lax.rsqrt(jnp.mean(o * o, axis=-1, keepdims=True) + EPS) * w


def _l2n(t):
    return t * lax.rsqrt(jnp.sum(t * t, axis=-1, keepdims=True) + EPS)


def _hgrn_gates(qa, fa, lb):
    logg = _logaddexp(jnp.log(lb), jnp.log1p(-lb) + _log_sigmoid(fa))
    k = (1.0 - lb) * _sigmoid(-fa)
    q = _silu(qa) * HEAD_DIM ** -0.5
    return q, k, logg


def _const_spec(shape):
    return pl.BlockSpec(shape, lambda *_: (0,) * len(shape))


def _layer_spec(shape, layer):
    return pl.BlockSpec((None,) + tuple(shape), lambda *_: (layer,) + (0,) * len(shape))


def _ffn_body(x_ref, nw_ref, wg_ref, wu_ref, wd_ref, fnw_ref, o_ref, acc_ref, *, ff_chunk, final):
    x = x_ref[...]
    h = _rms(x, nw_ref[...]).astype(BF16)
    d_ff = wg_ref.shape[1]
    for c in range(d_ff // ff_chunk):
        cs = slice(c * ff_chunk, (c + 1) * ff_chunk)
        g = _dot(h, wg_ref[:, cs])
        u = _dot(h, wu_ref[:, cs])
        a = (_silu(g) * u).astype(BF16)
        part = _dot(a, wd_ref[cs, :])
        if c == 0:
            acc_ref[...] = part
        else:
            acc_ref[...] += part
    y = x + 0.5 * acc_ref[...]
    if final:
        y = _rms(y, fnw_ref[...])
    o_ref[...] = y


def _ffn(x, nw, wg, wu, wd, fnw, layer, *, final, tm):
    n, d = x.shape
    d_ff = wg.shape[-1]
    tm = min(tm, n)
    body = functools.partial(_ffn_body, ff_chunk=256, final=final)
    return pl.pallas_call(
        body,
        grid=(n // tm,),
        in_specs=[
            pl.BlockSpec((tm, d), lambda i: (i, 0)),
            _layer_spec((1, d), layer),
            _layer_spec((d, d_ff), layer),
            _layer_spec((d, d_ff), layer),
            _layer_spec((d_ff, d), layer),
            _const_spec((1, d)),
        ],
        out_specs=pl.BlockSpec((tm, d), lambda i: (i, 0)),
        out_shape=jax.ShapeDtypeStruct((n, d), F32),
        scratch_shapes=[pltpu.VMEM((tm, d), F32)],
        compiler_params=pltpu.CompilerParams(
            dimension_semantics=("arbitrary",), vmem_limit_bytes=VMEM_LIMIT),
        name="ffn",
    )(x, nw, wg, wu, wd, fnw)


def _even_prompt_body(x_ref, nw_ref, win_ref, wab_ref, lb_ref, hgw_ref, convw_ref, alog_ref, dtb_ref,
                      gdw_ref, wout_ref,
                      xo_ref, shg_ref, sgd_ref, sconv_ref,
                      p_scr, ext_scr, o_scr, hgt_scr, vn_scr, *, tb):
    t = pl.program_id(1)
    nt = pl.num_programs(1)
    hd_w = HEAD_DIM
    mix = N_HEADS * hd_w
    tail = 8

    @pl.when(t == 0)
    def _init():
        hgt_scr[...] = jnp.zeros_like(hgt_scr)
        sgd_ref[...] = jnp.zeros_like(sgd_ref)
        ext_scr[0:tail, :] = jnp.zeros((tail, ext_scr.shape[1]), F32)

    x = x_ref[0]
    h = _rms(x, nw_ref[...]).astype(BF16)
    n_in = win_ref.shape[1]
    for c in range(n_in // 1024):
        cs = slice(c * 1024, (c + 1) * 1024)
        p_scr[:, cs] = _dot(h, win_ref[:, cs])
    ab = _dot(h, wab_ref[...])

    rows = lax.broadcasted_iota(jnp.int32, (tb, 1), 0)
    rr = lax.broadcasted_iota(jnp.int32, (tb, tb), 0)
    cc = lax.broadcasted_iota(jnp.int32, (tb, tb), 1)
    tril = (rr >= cc)

    cum_all = tril.astype(F32)
    for hd in range(N_HEADS):
        ls = slice(hd * hd_w, (hd + 1) * hd_w)
        qa = p_scr[:, ls]
        fa = p_scr[:, mix + hd * hd_w: mix + (hd + 1) * hd_w]
        v = p_scr[:, 2 * mix + hd * hd_w: 2 * mix + (hd + 1) * hd_w]
        og = p_scr[:, 3 * mix + hd * hd_w: 3 * mix + (hd + 1) * hd_w]
        q, k, logg = _hgrn_gates(qa, fa, lb_ref[:, ls])
        b = _dot_exact(cum_all, logg)

        r8 = rows & (HG_BAND - 1)
        o = jnp.sum(q * k, axis=-1, keepdims=True) * v
        for d in range(1, HG_BAND):
            e = jnp.where(r8 >= d, b - pltpu.roll(b, d, 0), NEG_INF)
            z = q * pltpu.roll(k, d, 0) * jnp.exp(e)
            o = o + jnp.sum(z, axis=-1, keepdims=True) * pltpu.roll(v, d, 0)

        att = jnp.zeros((tb, tb), F32)
        m = HG_BAND
        while m < tb:
            g2 = 2 * m
            bm = jnp.broadcast_to(b.reshape(tb // g2, g2, hd_w)[:, m - 1:m, :],
                                  (tb // g2, g2, hd_w)).reshape(tb, hd_w)
            second = (rows & (g2 - 1)) >= m
            qt = (q * jnp.exp(jnp.where(second, b - bm, NEG_INF))).astype(BF16)
            kt = (k * jnp.exp(jnp.where(second, NEG_INF, bm - b))).astype(BF16)
            am = _dot_nt(qt, kt)
            if g2 < tb:
                sh = g2.bit_length() - 1
                am = jnp.where((rr >> sh) == (cc >> sh), am, 0.0)
            att = att + am
            m = g2
        st = hgt_scr[hd]
        o = o + _dot(att.astype(BF16), v.astype(BF16))
        o = o + _dot_nt((q * jnp.exp(b)).astype(BF16), st.astype(BF16))
        bl = b[tb - 1:tb, :]
        kdec = (k * jnp.exp(bl - b)).astype(BF16)
        hgt_scr[hd] = st * jnp.exp(bl) + _dot_tn(v.astype(BF16), kdec)
        o = _head_rms(o, hgw_ref[...]) * _sigmoid(og)
        o_scr[:, ls] = o.astype(BF16)

    n_qkv = 3 * mix
    q0 = 4 * mix
    ext_scr[tail:tail + tb, :] = p_scr[:, q0:q0 + n_qkv]
    conv = ext_scr[tail:tail + tb, :] * convw_ref[GDN_CONV - 1:GDN_CONV, :]
    for j in range(1, GDN_CONV):
        conv = conv + ext_scr[tail - j:tail - j + tb, :] * convw_ref[GDN_CONV - 1 - j:GDN_CONV - j, :]
    keep = ext_scr[tb:tb + tail, :]
    ext_scr[0:tail, :] = keep
    conv = _silu(conv)

    g_all = -jnp.exp(alog_ref[...]) * _softplus(ab + dtb_ref[...])
    beta_all = _sigmoid(ab)
    csh = GDN_CHUNK.bit_length() - 1
    same = (rr >> csh) == (cc >> csh)
    incl = jnp.logical_and(same, tril)
    strict = jnp.logical_and(same, rr > cc)
    bcum = _dot_exact(incl.astype(F32), g_all)
    bcum_t = bcum.T
    eye = (rr == cc).astype(F32)
    n_chunks = tb // GDN_CHUNK
    for hd in range(N_HEADS):
        ls = slice(hd * hd_w, (hd + 1) * hd_w)
        qh = _l2n(conv[:, ls]) * hd_w ** -0.5
        kh = _l2n(conv[:, mix + hd * hd_w: mix + (hd + 1) * hd_w])
        vh = conv[:, 2 * mix + hd * hd_w: 2 * mix + (hd + 1) * hd_w]
        z = p_scr[:, q0 + n_qkv + hd * hd_w: q0 + n_qkv + (hd + 1) * hd_w]
        bcol = bcum[:, hd:hd + 1]
        brow = bcum_t[hd:hd + 1, :]
        beta = beta_all[:, N_HEADS + hd:N_HEADS + hd + 1]
        decay = jnp.exp(jnp.where(incl, bcol - brow, NEG_INF))
        kb = kh * beta
        kh16 = kh.astype(BF16)
        gram = _dot_nt(kb.astype(BF16), kh16)
        xm = -jnp.where(strict, gram * decay, 0.0)
        tinv = eye + xm
        y = xm
        span = 2
        while span < GDN_CHUNK:
            y = _dot(y.astype(BF16), y.astype(BF16))
            tinv = tinv + _dot(tinv.astype(BF16), y.astype(BF16))
            span *= 2
        rhs = jnp.concatenate([vh * beta, kb * jnp.exp(bcol)], axis=-1)
        sol = _dot(tinv.astype(BF16), rhs.astype(BF16))
        u = sol[:, :hd_w]
        w = sol[:, hd_w:]
        aqk = (_dot_nt(qh.astype(BF16), kh16) * decay).astype(BF16)
        qe = (qh * jnp.exp(bcol)).astype(BF16)
        vn_scr[...] = jnp.zeros_like(vn_scr)
        for c in range(n_chunks):
            rs = slice(c * GDN_CHUNK, (c + 1) * GDN_CHUNK)
            s = sgd_ref[0, hd]
            s16 = s.astype(BF16)
            vnew = u[rs] - _dot(w[rs].astype(BF16), s16)
            vn_scr[rs, :] = vnew
            o_c = _dot(qe[rs], s16) + _dot(aqk[rs], vn_scr[...].astype(BF16))
            bl = bcol[(c + 1) * GDN_CHUNK - 1:(c + 1) * GDN_CHUNK, :]
            kdec = (kh[rs] * jnp.exp(bl - bcol[rs])).astype(BF16)
            sgd_ref[0, hd] = s * jnp.exp(bl) + _dot_tn(kdec, vnew.astype(BF16))
            o_c = _head_rms(o_c, gdw_ref[...]) * _silu(z[rs])
            o_scr[rs, mix + hd * hd_w: mix + (hd + 1) * hd_w] = o_c.astype(BF16)

    xo_ref[0] = x + _dot(o_scr[...], wout_ref[...])

    @pl.when(t == nt - 1)
    def _fin():
        for hd in range(N_HEADS):
            shg_ref[0, hd] = hgt_scr[hd].T
        sconv_ref[0] = ext_scr[tail - (GDN_CONV - 1):tail, :]


def _even_prompt(x, nw, win, wab, lb, hgw, convw, alog, dtb, gdw, wout, layer, e, *, tb):
    bsz, seq, d = x.shape
    n_in = win.shape[-1]
    mix = N_HEADS * HEAD_DIM
    n_qkv = 3 * mix
    tb = min(tb, seq)
    body = functools.partial(_even_prompt_body, tb=tb)
    st_shape = (bsz, N_HEADS, HEAD_DIM, HEAD_DIM)
    return pl.pallas_call(
        body,
        grid=(bsz, seq // tb),
        in_specs=[
            pl.BlockSpec((1, tb, d), lambda b, t: (b, t, 0)),
            _layer_spec((1, d), layer),
            _layer_spec((d, n_in), e),
            _layer_spec((d, 128), e),
            _layer_spec((1, mix), e),
            _layer_spec((1, HEAD_DIM), e),
            _layer_spec((GDN_CONV, n_qkv), e),
            _layer_spec((1, 128), e),
            _layer_spec((1, 128), e),
            _layer_spec((1, HEAD_DIM), e),
            _layer_spec((d, d), e),
        ],
        out_specs=[
            pl.BlockSpec((1, tb, d), lambda b, t: (b, t, 0)),
            pl.BlockSpec((1, N_HEADS, HEAD_DIM, HEAD_DIM), lambda b, t: (b, 0, 0, 0)),
            pl.BlockSpec((1, N_HEADS, HEAD_DIM, HEAD_DIM), lambda b, t: (b, 0, 0, 0)),
            pl.BlockSpec((1, GDN_CONV - 1, n_qkv), lambda b, t: (b, 0, 0)),
        ],
        out_shape=[
            jax.ShapeDtypeStruct((bsz, seq, d), F32),
            jax.ShapeDtypeStruct(st_shape, F32),
            jax.ShapeDtypeStruct(st_shape, F32),
            jax.ShapeDtypeStruct((bsz, GDN_CONV - 1, n_qkv), F32),
        ],
        scratch_shapes=[
            pltpu.VMEM((tb, n_in), F32),
            pltpu.VMEM((tb + 8, n_qkv), F32),
            pltpu.VMEM((tb, d), BF16),
            pltpu.VMEM((N_HEADS, HEAD_DIM, HEAD_DIM), F32),
            pltpu.VMEM((tb, HEAD_DIM), F32),
        ],
        compiler_params=pltpu.CompilerParams(
            dimension_semantics=("arbitrary", "arbitrary"), vmem_limit_bytes=VMEM_LIMIT),
        name="even_prompt",
    )(x, nw, win, wab, lb, hgw, convw, alog, dtb, gdw, wout)


def _even_proj_body(x_ref, nw_ref, win_ref, wab_ref, p_ref, ab_ref):
    h = _rms(x_ref[...], nw_ref[...]).astype(BF16)
    n_in = win_ref.shape[1]
    for c in range(n_in // 1024):
        cs = slice(c * 1024, (c + 1) * 1024)
        p_ref[:, cs] = _dot(h, win_ref[:, cs])
    ab_ref[...] = _dot(h, wab_ref[...])


def _even_proj(x, nw, win, wab, layer, e):
    n, d = x.shape
    n_in = win.shape[-1]
    return pl.pallas_call(
        _even_proj_body,
        grid=(1,),
        in_specs=[
            _const_spec((n, d)),
            _layer_spec((1, d), layer),
            _layer_spec((d, n_in), e),
            _layer_spec((d, 128), e),
        ],
        out_specs=[_const_spec((n, n_in)), _const_spec((n, 128))],
        out_shape=[jax.ShapeDtypeStruct((n, n_in), F32), jax.ShapeDtypeStruct((n, 128), F32)],
        compiler_params=pltpu.CompilerParams(
            dimension_semantics=("arbitrary",), vmem_limit_bytes=VMEM_LIMIT),
        name="even_proj",
    )(x, nw, win, wab)


def _rows_to_cols(r):
    n = r.shape[0]
    if n < 128:
        r = jnp.concatenate([r, jnp.zeros((128 - n, r.shape[1]), r.dtype)], axis=0)
    return r.T


def _even_step_body(p_ref, ab_ref, shg_ref, sgd_ref, sconv_ref, lb_ref, hgw_ref, convw_ref, alog_ref,
                    dtb_ref, gdw_ref,
                    o_ref, shg_o, sgd_o, sconv_o, oa_scr, ob_scr, *, bb):
    hd_w = HEAD_DIM
    mix = N_HEADS * hd_w
    n_qkv = 3 * mix
    q0 = 4 * mix
    ab = ab_ref[...]
    g_all = -jnp.exp(alog_ref[...]) * _softplus(ab + dtb_ref[...])
    eg_all = jnp.exp(g_all)
    beta_all = _sigmoid(ab)

    qkv = p_ref[:, q0:q0 + n_qkv]
    conv = qkv * convw_ref[GDN_CONV - 1:GDN_CONV, :]
    for j in range(GDN_CONV - 1):
        conv = conv + sconv_ref[:, j, :] * convw_ref[j:j + 1, :]
    for j in range(GDN_CONV - 2):
        sconv_o[:, j, :] = sconv_ref[:, j + 1, :]
    sconv_o[:, GDN_CONV - 2, :] = qkv
    conv = _silu(conv)

    for hd in range(N_HEADS):
        ls = slice(hd * hd_w, (hd + 1) * hd_w)
        q, k, logg = _hgrn_gates(p_ref[:, ls], p_ref[:, mix + hd * hd_w: mix + (hd + 1) * hd_w],
                                 lb_ref[:, ls])
        v = p_ref[:, 2 * mix + hd * hd_w: 2 * mix + (hd + 1) * hd_w]
        qc = _rows_to_cols(q)
        kc = _rows_to_cols(k)
        gc = _rows_to_cols(jnp.exp(logg))
        q2 = _l2n(conv[:, ls]) * hd_w ** -0.5
        k2 = _l2n(conv[:, mix + hd * hd_w: mix + (hd + 1) * hd_w])
        v2 = conv[:, 2 * mix + hd * hd_w: 2 * mix + (hd + 1) * hd_w]
        q2c = _rows_to_cols(q2)
        k2c = _rows_to_cols(k2)
        for j in range(bb):
            s = shg_ref[j, hd]
            s_new = s * gc[:, j:j + 1] + kc[:, j:j + 1] * v[j:j + 1, :]
            shg_o[j, hd] = s_new
            oa_scr[j:j + 1, ls] = jnp.sum(s_new * qc[:, j:j + 1], axis=0, keepdims=True)

            s2 = sgd_ref[j, hd] * eg_all[j:j + 1, hd:hd + 1]
            ks = jnp.sum(s2 * k2c[:, j:j + 1], axis=0, keepdims=True)
            vnew = beta_all[j:j + 1, N_HEADS + hd:N_HEADS + hd + 1] * (v2[j:j + 1, :] - ks)
            s2_new = s2 + k2c[:, j:j + 1] * vnew
            sgd_o[j, hd] = s2_new
            ob_scr[j:j + 1, ls] = jnp.sum(s2_new * q2c[:, j:j + 1], axis=0, keepdims=True)

    for hd in range(N_HEADS):
        ls = slice(hd * hd_w, (hd + 1) * hd_w)
        og = p_ref[:, 3 * mix + hd * hd_w: 3 * mix + (hd + 1) * hd_w]
        z = p_ref[:, q0 + n_qkv + hd * hd_w: q0 + n_qkv + (hd + 1) * hd_w]
        o_ref[:, ls] = (_head_rms(oa_scr[:, ls], hgw_ref[...]) * _sigmoid(og)).astype(BF16)
        o_ref[:, mix + hd * hd_w: mix + (hd + 1) * hd_w] = (
            _head_rms(ob_scr[:, ls], gdw_ref[...]) * _silu(z)).astype(BF16)


def _even_step(p, ab, shg, sgd, sconv, lb, hgw, convw, alog, dtb, gdw, e, *, bb):
    n, n_in = p.shape
    mix = N_HEADS * HEAD_DIM
    n_qkv = 3 * mix
    d = 2 * mix
    body = functools.partial(_even_step_body, bb=bb)
    st_block = (None, bb, N_HEADS, HEAD_DIM, HEAD_DIM)
    st_spec = pl.BlockSpec(st_block, lambda i: (e, i, 0, 0, 0))
    st_ospec = pl.BlockSpec(st_block[1:], lambda i: (i, 0, 0, 0))
    cv_spec = pl.BlockSpec((None, bb, GDN_CONV - 1, n_qkv), lambda i: (e, i, 0, 0))
    cv_ospec = pl.BlockSpec((bb, GDN_CONV - 1, n_qkv), lambda i: (i, 0, 0))
    return pl.pallas_call(
        body,
        grid=(n // bb,),
        in_specs=[
            pl.BlockSpec((bb, n_in), lambda i: (i, 0)),
            pl.BlockSpec((bb, 128), lambda i: (i, 0)),
            st_spec, st_spec, cv_spec,
            _layer_spec((1, mix), e),
            _layer_spec((1, HEAD_DIM), e),
            _layer_spec((GDN_CONV, n_qkv), e),
            _layer_spec((1, 128), e),
            _layer_spec((1, 128), e),
            _layer_spec((1, HEAD_DIM), e),
        ],
        out_specs=[pl.BlockSpec((bb, d), lambda i: (i, 0)), st_ospec, st_ospec, cv_ospec],
        out_shape=[
            jax.ShapeDtypeStruct((n, d), BF16),
            jax.ShapeDtypeStruct(shg.shape[1:], F32),
            jax.ShapeDtypeStruct(sgd.shape[1:], F32),
            jax.ShapeDtypeStruct(sconv.shape[1:], F32),
        ],
        scratch_shapes=[pltpu.VMEM((bb, mix), F32), pltpu.VMEM((bb, mix), F32)],
        compiler_params=pltpu.CompilerParams(
            dimension_semantics=("arbitrary",), vmem_limit_bytes=VMEM_LIMIT),
        name="even_step",
    )(p, ab, shg, sgd, sconv, lb, hgw, convw, alog, dtb, gdw)


def _out_proj_body(o_ref, x_ref, w_ref, xo_ref):
    xo_ref[...] = x_ref[...] + _dot(o_ref[...], w_ref[...])


def _out_proj(o, x, wout, e):
    n, d = x.shape
    return pl.pallas_call(
        _out_proj_body,
        grid=(1,),
        in_specs=[_const_spec((n, d)), _const_spec((n, d)), _layer_spec((d, d), e)],
        out_specs=_const_spec((n, d)),
        out_shape=jax.ShapeDtypeStruct((n, d), F32),
        compiler_params=pltpu.CompilerParams(
            dimension_semantics=("arbitrary",), vmem_limit_bytes=VMEM_LIMIT),
        name="out_proj",
    )(o, x, wout)


def _pool_group_out(d_grp, g, wgrp_ref):
    return _dot(d_grp.astype(BF16), wgrp_ref[g])


def _pool_prompt_body(x_ref, nw_ref, win_ref, wgrp_ref, scale_ref, xo_ref, buf_ref, ext_scr, *, tb):
    t = pl.program_id(1)
    nt = pl.num_programs(1)
    carry = 16
    gd = ext_scr.shape[1] // len(POOL_WINDOWS)

    @pl.when(t == 0)
    def _init():
        ext_scr[0:carry, :] = jnp.zeros((carry, ext_scr.shape[1]), F32)

    x = x_ref[0]
    h = _rms(x, nw_ref[...]).astype(BF16)
    u = _dot(h, win_ref[...])
    ext_scr[carry:carry + tb, :] = u
    pos = t * tb + lax.broadcasted_iota(jnp.int32, (tb, 1), 0)
    for g, wlen in enumerate(POOL_WINDOWS):
        cs = slice(g * gd, (g + 1) * gd)
        s = ext_scr[:, cs]
        span = 1
        while span < wlen:
            s = s + pltpu.roll(s, span, 0)
            span *= 2
        cnt = jnp.minimum(pos + 1, wlen).astype(F32)
        dg = s[carry:carry + tb, :] / cnt - u[:, cs]
        y = _pool_group_out(dg, g, wgrp_ref) * scale_ref[:, cs]
        xo_ref[0, :, cs] = x[:, cs] + y
    keep = ext_scr[tb:tb + carry, :]
    ext_scr[0:carry, :] = keep

    @pl.when(t == nt - 1)
    def _fin():
        buf_ref[0] = ext_scr[carry - POOL_BUF:carry, :]


def _pool_prompt(x, nw, win, wgrp, scale, layer, o, *, tb):
    bsz, seq, d = x.shape
    gd = d // len(POOL_WINDOWS)
    tb = min(tb, seq)
    body = functools.partial(_pool_prompt_body, tb=tb)
    return pl.pallas_call(
        body,
        grid=(bsz, seq // tb),
        in_specs=[
            pl.BlockSpec((1, tb, d), lambda b, t: (b, t, 0)),
            _layer_spec((1, d), layer),
            _layer_spec((d, d), o),
            _layer_spec((len(POOL_WINDOWS), gd, gd), o),
            _layer_spec((1, d), o),
        ],
        out_specs=[
            pl.BlockSpec((1, tb, d), lambda b, t: (b, t, 0)),
            pl.BlockSpec((1, POOL_BUF, d), lambda b, t: (b, 0, 0)),
        ],
        out_shape=[
            jax.ShapeDtypeStruct((bsz, seq, d), F32),
            jax.ShapeDtypeStruct((bsz, POOL_BUF, d), F32),
        ],
        scratch_shapes=[pltpu.VMEM((tb + 16, d), F32)],
        compiler_params=pltpu.CompilerParams(
            dimension_semantics=("arbitrary", "arbitrary"), vmem_limit_bytes=VMEM_LIMIT),
        name="pool_prompt",
    )(x, nw, win, wgrp, scale)


def _pool_step_body(x_ref, buf_ref, nw_ref, win_ref, wgrp_ref, scale_ref, xo_ref, buf_o, u_scr, win_scr):
    n, d = x_ref.shape
    gd = d // len(POOL_WINDOWS)
    x = x_ref[...]
    h = _rms(x, nw_ref[...]).astype(BF16)
    u = _dot(h, win_ref[...])
    u_scr[...] = u
    lane_grp = lax.broadcasted_iota(jnp.int32, (POOL_BUF, d), 1) // gd
    rowi = lax.broadcasted_iota(jnp.int32, (POOL_BUF, d), 0)
    wlen = jnp.zeros((POOL_BUF, d), jnp.int32)
    for g, wl in enumerate(POOL_WINDOWS):
        wlen = jnp.where(lane_grp == g, wl, wlen)
    in_win = rowi >= (POOL_BUF + 1 - wlen)

    def per_seq(b, carry):
        rows = buf_ref[b]
        win_scr[pl.ds(b, 1), :] = jnp.sum(jnp.where(in_win, rows, 0.0), axis=0, keepdims=True)
        buf_o[b, 0:POOL_BUF - 1, :] = buf_ref[b, 1:POOL_BUF, :]
        buf_o[b, POOL_BUF - 1:POOL_BUF, :] = u_scr[pl.ds(b, 1), :]
        return carry

    lax.fori_loop(0, n, per_seq, 0)
    wsum = win_scr[...] + u
    for g, wl in enumerate(POOL_WINDOWS):
        cs = slice(g * gd, (g + 1) * gd)
        dg = wsum[:, cs] / float(wl) - u[:, cs]
        y = _pool_group_out(dg, g, wgrp_ref) * scale_ref[:, cs]
        xo_ref[:, cs] = x[:, cs] + y


def _pool_step(x, buf, nw, win, wgrp, scale, layer, o, *, pos0):
    assert pos0 + 1 >= max(POOL_WINDOWS)
    n, d = x.shape
    gd = d // len(POOL_WINDOWS)
    return pl.pallas_call(
        _pool_step_body,
        grid=(1,),
        in_specs=[
            _const_spec((n, d)),
            pl.BlockSpec((None, n, POOL_BUF, d), lambda i: (o, 0, 0, 0)),
            _layer_spec((1, d), layer),
            _layer_spec((d, d), o),
            _layer_spec((len(POOL_WINDOWS), gd, gd), o),
            _layer_spec((1, d), o),
        ],
        out_specs=[_const_spec((n, d)), _const_spec((n, POOL_BUF, d))],
        out_shape=[jax.ShapeDtypeStruct((n, d), F32), jax.ShapeDtypeStruct((n, POOL_BUF, d), F32)],
        scratch_shapes=[pltpu.VMEM((n, d), F32), pltpu.VMEM((n, d), F32)],
        compiler_params=pltpu.CompilerParams(
            dimension_semantics=("arbitrary",), vmem_limit_bytes=VMEM_LIMIT),
        name="pool_step",
    )(x, buf, nw, win, wgrp, scale)


def _pad_lanes(a, width=128):
    return jnp.pad(a, [(0, 0)] * (a.ndim - 1) + [(0, width - a.shape[-1])])


def kernel(x_prompt, x_sample, state_hgrn, state_gdn, state_gdn_conv, state_pool, hgrn_lb, w_in_even, hgrn_norm_w, gdn_conv_w, gdn_a_log, gdn_dt_bias, gdn_norm_w, w_out_even, pool_w_in, pool_w_group, pool_scale, norm_ffn1, ffn1_w_gate, ffn1_w_up, ffn1_w_down, norm_mix, norm_ffn2, ffn2_w_gate, ffn2_w_up, ffn2_w_down, final_norm):
    depth, d = norm_mix.shape
    bsz, seq, _ = x_prompt.shape
    n_s, dec_seq, _ = x_sample.shape
    assert dec_seq == 1
    mix = N_HEADS * HEAD_DIM
    n_main = 8 * mix
    past_len = 16384

    bf = lambda a: a.astype(BF16)
    win_main = bf(w_in_even[:, :, :n_main])
    win_ab = bf(_pad_lanes(w_in_even[:, :, n_main:]))
    wout = bf(w_out_even)
    pwin, pwgrp = bf(pool_w_in), bf(pool_w_group)
    f1 = (bf(ffn1_w_gate), bf(ffn1_w_up), bf(ffn1_w_down))
    f2 = (bf(ffn2_w_gate), bf(ffn2_w_up), bf(ffn2_w_down))
    lb_all = jnp.cumsum(jax.nn.softmax(hgrn_lb.astype(F32), axis=0), axis=0)
    lb_all = (lb_all - lb_all[:1])[:, None, :]
    row = lambda a: a[:, None, :]
    nf1, nmx, nf2 = row(norm_ffn1), row(norm_mix), row(norm_ffn2)
    hgw, gdw = row(hgrn_norm_w), row(gdn_norm_w)
    alog, dtb = row(_pad_lanes(gdn_a_log)), row(_pad_lanes(gdn_dt_bias))
    pscale = row(pool_scale)
    fnw = final_norm[None, :]

    xp = x_prompt
    xs = x_sample.reshape(n_s, d)
    hg_p, gd_p, cv_p, pl_p = [], [], [], []
    hg_s, gd_s, cv_s, pl_s = [], [], [], []
    for li in range(depth):
        last = li == depth - 1
        xp = _ffn(xp.reshape(bsz * seq, d), nf1, *f1, fnw, li, final=False, tm=512).reshape(bsz, seq, d)
        xs = _ffn(xs, nf1, *f1, fnw, li, final=False, tm=512)
        if li % 2 == 0:
            e = li // 2
            xp, a, b, c = _even_prompt(xp, nmx, win_main, win_ab, lb_all, hgw, gdn_conv_w, alog, dtb,
                                       gdw, wout, li, e, tb=256)
            hg_p.append(a), gd_p.append(b), cv_p.append(c)
            p, ab = _even_proj(xs, nmx, win_main, win_ab, li, e)
            o, a, b, c = _even_step(p, ab, state_hgrn, state_gdn, state_gdn_conv, lb_all, hgw,
                                    gdn_conv_w, alog, dtb, gdw, e, bb=8)
            xs = _out_proj(o, xs, wout, e)
            hg_s.append(a), gd_s.append(b), cv_s.append(c)
        else:
            o = li // 2
            xp, pb = _pool_prompt(xp, nmx, pwin, pwgrp, pscale, li, o, tb=512)
            pl_p.append(pb)
            xs, pb = _pool_step(xs, state_pool, nmx, pwin, pwgrp, pscale, li, o, pos0=past_len)
            pl_s.append(pb)
        xp = _ffn(xp.reshape(bsz * seq, d), nf2, *f2, fnw, li, final=last, tm=512).reshape(bsz, seq, d)
        xs = _ffn(xs, nf2, *f2, fnw, li, final=last, tm=512)

    st = jnp.stack
    return (xp, xs.reshape(n_s, 1, d), st(hg_p), st(gd_p), st(cv_p), st(pl_p),
            st(hg_s), st(gd_s), st(cv_s), st(pl_s))
```

```python
import functools

import jax
import jax.numpy as jnp
from jax import lax
from jax.experimental import pallas as pl
from jax.experimental.pallas import tpu as pltpu

F32 = jnp.float32
BF16 = jnp.bfloat16
EPS = 1e-6
HEAD_DIM = 128
N_HEADS = 4
GDN_CHUNK = 64
HG_BAND = 4
POOL_WINDOWS = (2, 4, 8, 16)
POOL_BUF = max(POOL_WINDOWS) - 1
GDN_CONV = 4
VMEM_LIMIT = 56 * 1024 * 1024
NEG_INF = float("-inf")


def _dot(a, b):
    return jnp.dot(a, b, preferred_element_type=F32)


def _dot_nt(a, b):
    return lax.dot_general(a, b, (((1,), (1,)), ((), ())), preferred_element_type=F32)


def _dot_tn(a, b):
    return lax.dot_general(a, b, (((0,), (0,)), ((), ())), preferred_element_type=F32)


def _masked_rowsum(mask16, x):
    hi = x.astype(BF16)
    r1 = x - hi.astype(F32)
    mid = r1.astype(BF16)
    lo = (r1 - mid.astype(F32)).astype(BF16)
    n = x.shape[1]
    y = _dot(mask16, jnp.concatenate([hi, mid, lo], axis=1))
    return y[:, :n] + y[:, n:2 * n] + y[:, 2 * n:]


def _rms(x, w):
    return x * lax.rsqrt(jnp.mean(x * x, axis=-1, keepdims=True) + EPS) * w


def _sigmoid(x):
    return 1.0 / (1.0 + jnp.exp(-x))


def _silu(x):
    return x * _sigmoid(x)


def _softplus(x):
    return jnp.maximum(x, 0.0) + jnp.log1p(jnp.exp(-jnp.abs(x)))


def _log_sigmoid(x):
    return jnp.minimum(x, 0.0) - jnp.log1p(jnp.exp(-jnp.abs(x)))


def _logaddexp(a, b):
    return jnp.maximum(a, b) + jnp.log1p(jnp.exp(-jnp.abs(a - b)))


def _head_rms(o, w):
    return o * lax.rsqrt(jnp.mean(o * o, axis=-1, keepdims=True) + EPS) * w


def _l2n(t):
    return t * lax.rsqrt(jnp.sum(t * t, axis=-1, keepdims=True) + EPS)


def _hgrn_gates(qa, fa, lb):
    logg = _logaddexp(jnp.log(lb), jnp.log1p(-lb) + _log_sigmoid(fa))
    k = (1.0 - lb) * _sigmoid(-fa)
    q = _silu(qa) * HEAD_DIM ** -0.5
    return q, k, logg


def _const_spec(shape):
    return pl.BlockSpec(shape, lambda *_: (0,) * len(shape))


def _layer_spec(shape, layer):
    return pl.BlockSpec((None,) + tuple(shape), lambda *_: (layer,) + (0,) * len(shape))


def _ffn_body(x_ref, nw_ref, wg_ref, wu_ref, wd_ref, fnw_ref, o_ref, acc_ref, *, ff_chunk, final):
    x = x_ref[...]
    h = _rms(x, nw_ref[...]).astype(BF16)
    d_ff = wg_ref.shape[1]
    for c in range(d_ff // ff_chunk):
        cs = slice(c * ff_chunk, (c + 1) * ff_chunk)
        g = _dot(h, wg_ref[:, cs])
        u = _dot(h, wu_ref[:, cs])
        a = (_silu(g) * u).astype(BF16)
        part = _dot(a, wd_ref[cs, :])
        if c == 0:
            acc_ref[...] = part
        else:
            acc_ref[...] += part
    y = x + 0.5 * acc_ref[...]
    if final:
        y = _rms(y, fnw_ref[...])
    o_ref[...] = y


def _ffn(x, nw, wg, wu, wd, fnw, layer, *, final, tm):
    n, d = x.shape
    d_ff = wg.shape[-1]
    tm = min(tm, n)
    body = functools.partial(_ffn_body, ff_chunk=256, final=final)
    return pl.pallas_call(
        body,
        grid=(n // tm,),
        in_specs=[
            pl.BlockSpec((tm, d), lambda i: (i, 0)),
            _layer_spec((1, d), layer),
            _layer_spec((d, d_ff), layer),
            _layer_spec((d, d_ff), layer),
            _layer_spec((d_ff, d), layer),
            _const_spec((1, d)),
        ],
        out_specs=pl.BlockSpec((tm, d), lambda i: (i, 0)),
        out_shape=jax.ShapeDtypeStruct((n, d), F32),
        scratch_shapes=[pltpu.VMEM((tm, d), F32)],
        compiler_params=pltpu.CompilerParams(
            dimension_semantics=("arbitrary",), vmem_limit_bytes=VMEM_LIMIT),
        name="ffn",
    )(x, nw, wg, wu, wd, fnw)


def _even_prompt_body(x_ref, nw_ref, win_ref, wab_ref, lb_ref, hgw_ref, convw_ref, alog_ref, dtb_ref,
                      gdw_ref, wout_ref,
                      xo_ref, shg_ref, sgd_ref, sconv_ref,
                      p_scr, ext_scr, o_scr, hgt_scr, vn_scr, *, tb):
    t = pl.program_id(1)
    nt = pl.num_programs(1)
    hd_w = HEAD_DIM
    mix = N_HEADS * hd_w
    tail = 8

    @pl.when(t == 0)
    def _init():
        hgt_scr[...] = jnp.zeros_like(hgt_scr)
        sgd_ref[...] = jnp.zeros_like(sgd_ref)
        ext_scr[0:tail, :] = jnp.zeros((tail, ext_scr.shape[1]), F32)

    x = x_ref[0]
    h = _rms(x, nw_ref[...]).astype(BF16)
    n_in = win_ref.shape[1]
    for c in range(n_in // 1024):
        cs = slice(c * 1024, (c + 1) * 1024)
        p_scr[:, cs] = _dot(h, win_ref[:, cs])
    ab = _dot(h, wab_ref[...])

    rows = lax.broadcasted_iota(jnp.int32, (tb, 1), 0)
    rr = lax.broadcasted_iota(jnp.int32, (tb, tb), 0)
    cc = lax.broadcasted_iota(jnp.int32, (tb, tb), 1)
    tril = (rr >= cc)

    tril16 = tril.astype(BF16)
    q, k, logg = _hgrn_gates(p_scr[:, 0:mix], p_scr[:, mix:2 * mix], lb_ref[...])
    v = p_scr[:, 2 * mix:3 * mix]
    b = _masked_rowsum(tril16, logg)
    v16 = v.astype(BF16)

    rb = rows & (HG_BAND - 1)
    zs = [q * k]
    vs = [v]
    for d in range(1, HG_BAND):
        e = jnp.where(rb >= d, b - pltpu.roll(b, d, 0), NEG_INF)
        zs.append(q * pltpu.roll(k, d, 0) * jnp.exp(e))
        vs.append(pltpu.roll(v, d, 0))

    levels = []
    m = HG_BAND
    while m < tb:
        g2 = 2 * m
        bm = jnp.broadcast_to(b.reshape(tb // g2, g2, mix)[:, m - 1:m, :],
                              (tb // g2, g2, mix)).reshape(tb, mix)
        second = (rows & (g2 - 1)) >= m
        e = jnp.exp(-jnp.abs(b - bm))
        qt = jnp.where(second, q * e, 0.0).astype(BF16)
        kt = jnp.where(second, 0.0, k * e).astype(BF16)
        levels.append((g2, qt, kt))
        m = g2
    qs = (q * jnp.exp(b)).astype(BF16)
    bl = b[tb - 1:tb, :]
    kdec = (k * jnp.exp(bl - b)).astype(BF16)
    ebl = jnp.exp(bl)

    for hd in range(N_HEADS):
        ls = slice(hd * hd_w, (hd + 1) * hd_w)
        o = jnp.sum(zs[0][:, ls], axis=-1, keepdims=True) * vs[0][:, ls]
        for d in range(1, HG_BAND):
            o = o + jnp.sum(zs[d][:, ls], axis=-1, keepdims=True) * vs[d][:, ls]
        att = None
        for g2, qt, kt in levels:
            am = _dot_nt(qt[:, ls], kt[:, ls])
            if g2 < tb:
                sh = g2.bit_length() - 1
                am = jnp.where((rr >> sh) == (cc >> sh), am, 0.0)
            att = am if att is None else att + am
        st = hgt_scr[hd]
        o = o + _dot(att.astype(BF16), v16[:, ls])
        o = o + _dot_nt(qs[:, ls], st.astype(BF16))
        hgt_scr[hd] = st * ebl[:, ls] + _dot_tn(v16[:, ls], kdec[:, ls])
        og = p_scr[:, 3 * mix + hd * hd_w: 3 * mix + (hd + 1) * hd_w]
        o = _head_rms(o, hgw_ref[...]) * _sigmoid(og)
        o_scr[:, ls] = o.astype(BF16)

    n_qkv = 3 * mix
    q0 = 4 * mix
    ext_scr[tail:tail + tb, :] = p_scr[:, q0:q0 + n_qkv]
    conv = ext_scr[tail:tail + tb, :] * convw_ref[GDN_CONV - 1:GDN_CONV, :]
    for j in range(1, GDN_CONV):
        conv = conv + ext_scr[tail - j:tail - j + tb, :] * convw_ref[GDN_CONV - 1 - j:GDN_CONV - j, :]
    keep = ext_scr[tb:tb + tail, :]
    ext_scr[0:tail, :] = keep
    conv = _silu(conv)

    g_all = -jnp.exp(alog_ref[...]) * _softplus(ab + dtb_ref[...])
    beta_all = _sigmoid(ab)
    csh = GDN_CHUNK.bit_length() - 1
    same = (rr >> csh) == (cc >> csh)
    incl = jnp.logical_and(same, tril)
    strict = jnp.logical_and(same, rr > cc)
    bcum = _masked_rowsum(incl.astype(BF16), g_all)
    bcum_t = bcum.T
    eye = (rr == cc).astype(F32)
    n_chunks = tb // GDN_CHUNK
    heads = range(N_HEADS)
    kh, kh16, vh, bcol, aqk, qe, tinv, ys, rhs = [], [], [], [], [], [], [], [], []
    for hd in heads:
        ls = slice(hd * hd_w, (hd + 1) * hd_w)
        qh = _l2n(conv[:, ls]) * hd_w ** -0.5
        kh.append(_l2n(conv[:, mix + hd * hd_w: mix + (hd + 1) * hd_w]))
        vh = conv[:, 2 * mix + hd * hd_w: 2 * mix + (hd + 1) * hd_w]
        bcol.append(bcum[:, hd:hd + 1])
        brow = bcum_t[hd:hd + 1, :]
        beta = beta_all[:, N_HEADS + hd:N_HEADS + hd + 1]
        decay = jnp.exp(jnp.where(incl, bcol[hd] - brow, NEG_INF))
        kb = kh[hd] * beta
        kh16.append(kh[hd].astype(BF16))
        gram = _dot_nt(kb.astype(BF16), kh16[hd])
        xm = -jnp.where(strict, gram * decay, 0.0)
        ys.append(xm)
        tinv.append(eye + xm)
        rhs.append(jnp.concatenate([vh * beta, kb * jnp.exp(bcol[hd])], axis=-1).astype(BF16))
        aqk.append((_dot_nt(qh.astype(BF16), kh16[hd]) * decay).astype(BF16))
        qe.append((qh * jnp.exp(bcol[hd])).astype(BF16))
    span = 2
    while span < GDN_CHUNK:
        for hd in heads:
            y16 = ys[hd].astype(BF16)
            ys[hd] = _dot(y16, y16)
        for hd in heads:
            tinv[hd] = tinv[hd] + _dot(tinv[hd].astype(BF16), ys[hd].astype(BF16))
        span *= 2
    us, ws = [], []
    for hd in heads:
        sol = _dot(tinv[hd].astype(BF16), rhs[hd])
        us.append(sol[:, :hd_w])
        ws.append(sol[:, hd_w:].astype(BF16))
    vn_scr[...] = jnp.zeros_like(vn_scr)
    for c in range(n_chunks):
        rs = slice(c * GDN_CHUNK, (c + 1) * GDN_CHUNK)
        for hd in heads:
            s = sgd_ref[0, hd]
            s16 = s.astype(BF16)
            vnew = us[hd][rs] - _dot(ws[hd][rs], s16)
            vn_scr[hd, rs, :] = vnew
            o_c = _dot(qe[hd][rs], s16) + _dot(aqk[hd][rs], vn_scr[hd].astype(BF16))
            bl = bcol[hd][(c + 1) * GDN_CHUNK - 1:(c + 1) * GDN_CHUNK, :]
            kdec = (kh[hd][rs] * jnp.exp(bl - bcol[hd][rs])).astype(BF16)
            sgd_ref[0, hd] = s * jnp.exp(bl) + _dot_tn(kdec, vnew.astype(BF16))
            z = p_scr[rs, q0 + n_qkv + hd * hd_w: q0 + n_qkv + (hd + 1) * hd_w]
            o_c = _head_rms(o_c, gdw_ref[...]) * _silu(z)
            o_scr[rs, mix + hd * hd_w: mix + (hd + 1) * hd_w] = o_c.astype(BF16)

    xo_ref[0] = x + _dot(o_scr[...], wout_ref[...])

    @pl.when(t == nt - 1)
    def _fin():
        for hd in range(N_HEADS):
            shg_ref[0, hd] = hgt_scr[hd].T
        sconv_ref[0] = ext_scr[tail - (GDN_CONV - 1):tail, :]


def _even_prompt(x, nw, win, wab, lb, hgw, convw, alog, dtb, gdw, wout, layer, e, *, tb):
    bsz, seq, d = x.shape
    n_in = win.shape[-1]
    mix = N_HEADS * HEAD_DIM
    n_qkv = 3 * mix
    tb = min(tb, seq)
    body = functools.partial(_even_prompt_body, tb=tb)
    st_shape = (bsz, N_HEADS, HEAD_DIM, HEAD_DIM)
    return pl.pallas_call(
        body,
        grid=(bsz, seq // tb),
        in_specs=[
            pl.BlockSpec((1, tb, d), lambda b, t: (b, t, 0)),
            _layer_spec((1, d), layer),
            _layer_spec((d, n_in), e),
            _layer_spec((d, 128), e),
            _layer_spec((1, mix), e),
            _layer_spec((1, HEAD_DIM), e),
            _layer_spec((GDN_CONV, n_qkv), e),
            _layer_spec((1, 128), e),
            _layer_spec((1, 128), e),
            _layer_spec((1, HEAD_DIM), e),
            _layer_spec((d, d), e),
        ],
        out_specs=[
            pl.BlockSpec((1, tb, d), lambda b, t: (b, t, 0)),
            pl.BlockSpec((1, N_HEADS, HEAD_DIM, HEAD_DIM), lambda b, t: (b, 0, 0, 0)),
            pl.BlockSpec((1, N_HEADS, HEAD_DIM, HEAD_DIM), lambda b, t: (b, 0, 0, 0)),
            pl.BlockSpec((1, GDN_CONV - 1, n_qkv), lambda b, t: (b, 0, 0)),
        ],
        out_shape=[
            jax.ShapeDtypeStruct((bsz, seq, d), F32),
            jax.ShapeDtypeStruct(st_shape, F32),
            jax.ShapeDtypeStruct(st_shape, F32),
            jax.ShapeDtypeStruct((bsz, GDN_CONV - 1, n_qkv), F32),
        ],
        scratch_shapes=[
            pltpu.VMEM((tb, n_in), F32),
            pltpu.VMEM((tb + 8, n_qkv), F32),
            pltpu.VMEM((tb, d), BF16),
            pltpu.VMEM((N_HEADS, HEAD_DIM, HEAD_DIM), F32),
            pltpu.VMEM((N_HEADS, tb, HEAD_DIM), F32),
        ],
        compiler_params=pltpu.CompilerParams(
            dimension_semantics=("arbitrary", "arbitrary"), vmem_limit_bytes=VMEM_LIMIT),
        name="even_prompt",
    )(x, nw, win, wab, lb, hgw, convw, alog, dtb, gdw, wout)


def _even_proj_body(x_ref, nw_ref, win_ref, wab_ref, p_ref, ab_ref):
    h = _rms(x_ref[...], nw_ref[...]).astype(BF16)
    n_in = win_ref.shape[1]
    for c in range(n_in // 1024):
        cs = slice(c * 1024, (c + 1) * 1024)
        p_ref[:, cs] = _dot(h, win_ref[:, cs])
    ab_ref[...] = _dot(h, wab_ref[...])


def _even_proj(x, nw, win, wab, layer, e):
    n, d = x.shape
    n_in = win.shape[-1]
    return pl.pallas_call(
        _even_proj_body,
        grid=(1,),
        in_specs=[
            _const_spec((n, d)),
            _layer_spec((1, d), layer),
            _layer_spec((d, n_in), e),
            _layer_spec((d, 128), e),
        ],
        out_specs=[_const_spec((n, n_in)), _const_spec((n, 128))],
        out_shape=[jax.ShapeDtypeStruct((n, n_in), F32), jax.ShapeDtypeStruct((n, 128), F32)],
        compiler_params=pltpu.CompilerParams(
            dimension_semantics=("arbitrary",), vmem_limit_bytes=VMEM_LIMIT),
        name="even_proj",
    )(x, nw, win, wab)


def _rows_to_cols(r):
    n = r.shape[0]
    if n < 128:
        r = jnp.concatenate([r, jnp.zeros((128 - n, r.shape[1]), r.dtype)], axis=0)
    return r.T


def _even_step_body(p_ref, ab_ref, shg_ref, sgd_ref, sconv_ref, lb_ref, hgw_ref, convw_ref, alog_ref,
                    dtb_ref, gdw_ref,
                    o_ref, shg_o, sgd_o, sconv_o, oa_scr, ob_scr, *, bb):
    hd_w = HEAD_DIM
    mix = N_HEADS * hd_w
    n_qkv = 3 * mix
    q0 = 4 * mix
    ab = ab_ref[...]
    g_all = -jnp.exp(alog_ref[...]) * _softplus(ab + dtb_ref[...])
    eg_all = jnp.exp(g_all)
    beta_all = _sigmoid(ab)

    qkv = p_ref[:, q0:q0 + n_qkv]
    conv = qkv * convw_ref[GDN_CONV - 1:GDN_CONV, :]
    for j in range(GDN_CONV - 1):
        conv = conv + sconv_ref[:, j, :] * convw_ref[j:j + 1, :]
    for j in range(GDN_CONV - 2):
        sconv_o[:, j, :] = sconv_ref[:, j + 1, :]
    sconv_o[:, GDN_CONV - 2, :] = qkv
    conv = _silu(conv)

    for hd in range(N_HEADS):
        ls = slice(hd * hd_w, (hd + 1) * hd_w)
        q, k, logg = _hgrn_gates(p_ref[:, ls], p_ref[:, mix + hd * hd_w: mix + (hd + 1) * hd_w],
                                 lb_ref[:, ls])
        v = p_ref[:, 2 * mix + hd * hd_w: 2 * mix + (hd + 1) * hd_w]
        qc = _rows_to_cols(q)
        kc = _rows_to_cols(k)
        gc = _rows_to_cols(jnp.exp(logg))
        q2 = _l2n(conv[:, ls]) * hd_w ** -0.5
        k2 = _l2n(conv[:, mix + hd * hd_w: mix + (hd + 1) * hd_w])
        v2 = conv[:, 2 * mix + hd * hd_w: 2 * mix + (hd + 1) * hd_w]
        q2c = _rows_to_cols(q2)
        k2c = _rows_to_cols(k2)
        for j in range(bb):
            s = shg_ref[j, hd]
            s_new = s * gc[:, j:j + 1] + kc[:, j:j + 1] * v[j:j + 1, :]
            shg_o[j, hd] = s_new
            oa_scr[j:j + 1, ls] = jnp.sum(s_new * qc[:, j:j + 1], axis=0, keepdims=True)

            s2 = sgd_ref[j, hd] * eg_all[j:j + 1, hd:hd + 1]
            ks = jnp.sum(s2 * k2c[:, j:j + 1], axis=0, keepdims=True)
            vnew = beta_all[j:j + 1, N_HEADS + hd:N_HEADS + hd + 1] * (v2[j:j + 1, :] - ks)
            s2_new = s2 + k2c[:, j:j + 1] * vnew
            sgd_o[j, hd] = s2_new
            ob_scr[j:j + 1, ls] = jnp.sum(s2_new * q2c[:, j:j + 1], axis=0, keepdims=True)

    for hd in range(N_HEADS):
        ls = slice(hd * hd_w, (hd + 1) * hd_w)
        og = p_ref[:, 3 * mix + hd * hd_w: 3 * mix + (hd + 1) * hd_w]
        z = p_ref[:, q0 + n_qkv + hd * hd_w: q0 + n_qkv + (hd + 1) * hd_w]
        o_ref[:, ls] = (_head_rms(oa_scr[:, ls], hgw_ref[...]) * _sigmoid(og)).astype(BF16)
        o_ref[:, mix + hd * hd_w: mix + (hd + 1) * hd_w] = (
            _head_rms(ob_scr[:, ls], gdw_ref[...]) * _silu(z)).astype(BF16)


def _even_step(p, ab, shg, sgd, sconv, lb, hgw, convw, alog, dtb, gdw, e, *, bb):
    n, n_in = p.shape
    mix = N_HEADS * HEAD_DIM
    n_qkv = 3 * mix
    d = 2 * mix
    body = functools.partial(_even_step_body, bb=bb)
    st_block = (None, bb, N_HEADS, HEAD_DIM, HEAD_DIM)
    st_spec = pl.BlockSpec(st_block, lambda i: (e, i, 0, 0, 0))
    st_ospec = pl.BlockSpec(st_block[1:], lambda i: (i, 0, 0, 0))
    cv_spec = pl.BlockSpec((None, bb, GDN_CONV - 1, n_qkv), lambda i: (e, i, 0, 0))
    cv_ospec = pl.BlockSpec((bb, GDN_CONV - 1, n_qkv), lambda i: (i, 0, 0))
    return pl.pallas_call(
        body,
        grid=(n // bb,),
        in_specs=[
            pl.BlockSpec((bb, n_in), lambda i: (i, 0)),
            pl.BlockSpec((bb, 128), lambda i: (i, 0)),
            st_spec, st_spec, cv_spec,
            _layer_spec((1, mix), e),
            _layer_spec((1, HEAD_DIM), e),
            _layer_spec((GDN_CONV, n_qkv), e),
            _layer_spec((1, 128), e),
            _layer_spec((1, 128), e),
            _layer_spec((1, HEAD_DIM), e),
        ],
        out_specs=[pl.BlockSpec((bb, d), lambda i: (i, 0)), st_ospec, st_ospec, cv_ospec],
        out_shape=[
            jax.ShapeDtypeStruct((n, d), BF16),
            jax.ShapeDtypeStruct(shg.shape[1:], F32),
            jax.ShapeDtypeStruct(sgd.shape[1:], F32),
            jax.ShapeDtypeStruct(sconv.shape[1:], F32),
        ],
        scratch_shapes=[pltpu.VMEM((bb, mix), F32), pltpu.VMEM((bb, mix), F32)],
        compiler_params=pltpu.CompilerParams(
            dimension_semantics=("arbitrary",), vmem_limit_bytes=VMEM_LIMIT),
        name="even_step",
    )(p, ab, shg, sgd, sconv, lb, hgw, convw, alog, dtb, gdw)


def _out_proj_body(o_ref, x_ref, w_ref, xo_ref):
    xo_ref[...] = x_ref[...] + _dot(o_ref[...], w_ref[...])


def _out_proj(o, x, wout, e):
    n, d = x.shape
    return pl.pallas_call(
        _out_proj_body,
        grid=(1,),
        in_specs=[_const_spec((n, d)), _const_spec((n, d)), _layer_spec((d, d), e)],
        out_specs=_const_spec((n, d)),
        out_shape=jax.ShapeDtypeStruct((n, d), F32),
        compiler_params=pltpu.CompilerParams(
            dimension_semantics=("arbitrary",), vmem_limit_bytes=VMEM_LIMIT),
        name="out_proj",
    )(o, x, wout)


def _pool_group_out(d_grp, g, wgrp_ref):
    return _dot(d_grp.astype(BF16), wgrp_ref[g])


def _pool_prompt_body(x_ref, nw_ref, win_ref, wgrp_ref, scale_ref, xo_ref, buf_ref, ext_scr, *, tb):
    t = pl.program_id(1)
    nt = pl.num_programs(1)
    carry = 16
    gd = ext_scr.shape[1] // len(POOL_WINDOWS)

    @pl.when(t == 0)
    def _init():
        ext_scr[0:carry, :] = jnp.zeros((carry, ext_scr.shape[1]), F32)

    x = x_ref[0]
    h = _rms(x, nw_ref[...]).astype(BF16)
    u = _dot(h, win_ref[...])
    ext_scr[carry:carry + tb, :] = u
    pos = t * tb + lax.broadcasted_iota(jnp.int32, (tb, 1), 0)
    for g, wlen in enumerate(POOL_WINDOWS):
        cs = slice(g * gd, (g + 1) * gd)
        s = ext_scr[:, cs]
        span = 1
        while span < wlen:
            s = s + pltpu.roll(s, span, 0)
            span *= 2
        cnt = jnp.minimum(pos + 1, wlen).astype(F32)
        dg = s[carry:carry + tb, :] / cnt - u[:, cs]
        y = _pool_group_out(dg, g, wgrp_ref) * scale_ref[:, cs]
        xo_ref[0, :, cs] = x[:, cs] + y
    keep = ext_scr[tb:tb + carry, :]
    ext_scr[0:carry, :] = keep

    @pl.when(t == nt - 1)
    def _fin():
        buf_ref[0] = ext_scr[carry - POOL_BUF:carry, :]


def _pool_prompt(x, nw, win, wgrp, scale, layer, o, *, tb):
    bsz, seq, d = x.shape
    gd = d // len(POOL_WINDOWS)
    tb = min(tb, seq)
    body = functools.partial(_pool_prompt_body, tb=tb)
    return pl.pallas_call(
        body,
        grid=(bsz, seq // tb),
        in_specs=[
            pl.BlockSpec((1, tb, d), lambda b, t: (b, t, 0)),
            _layer_spec((1, d), layer),
            _layer_spec((d, d), o),
            _layer_spec((len(POOL_WINDOWS), gd, gd), o),
            _layer_spec((1, d), o),
        ],
        out_specs=[
            pl.BlockSpec((1, tb, d), lambda b, t: (b, t, 0)),
            pl.BlockSpec((1, POOL_BUF, d), lambda b, t: (b, 0, 0)),
        ],
        out_shape=[
            jax.ShapeDtypeStruct((bsz, seq, d), F32),
            jax.ShapeDtypeStruct((bsz, POOL_BUF, d), F32),
        ],
        scratch_shapes=[pltpu.VMEM((tb + 16, d), F32)],
        compiler_params=pltpu.CompilerParams(
            dimension_semantics=("arbitrary", "arbitrary"), vmem_limit_bytes=VMEM_LIMIT),
        name="pool_prompt",
    )(x, nw, win, wgrp, scale)


def _pool_step_body(x_ref, buf_ref, nw_ref, win_ref, wgrp_ref, scale_ref, xo_ref, buf_o, u_scr, win_scr):
    n, d = x_ref.shape
    gd = d // len(POOL_WINDOWS)
    x = x_ref[...]
    h = _rms(x, nw_ref[...]).astype(BF16)
    u = _dot(h, win_ref[...])
    u_scr[...] = u
    lane_grp = lax.broadcasted_iota(jnp.int32, (POOL_BUF, d), 1) // gd
    rowi = lax.broadcasted_iota(jnp.int32, (POOL_BUF, d), 0)
    wlen = jnp.zeros((POOL_BUF, d), jnp.int32)
    for g, wl in enumerate(POOL_WINDOWS):
        wlen = jnp.where(lane_grp == g, wl, wlen)
    in_win = rowi >= (POOL_BUF + 1 - wlen)

    def per_seq(b, carry):
        rows = buf_ref[b]
        win_scr[pl.ds(b, 1), :] = jnp.sum(jnp.where(in_win, rows, 0.0), axis=0, keepdims=True)
        buf_o[b, 0:POOL_BUF - 1, :] = buf_ref[b, 1:POOL_BUF, :]
        buf_o[b, POOL_BUF - 1:POOL_BUF, :] = u_scr[pl.ds(b, 1), :]
        return carry

    lax.fori_loop(0, n, per_seq, 0)
    wsum = win_scr[...] + u
    for g, wl in enumerate(POOL_WINDOWS):
        cs = slice(g * gd, (g + 1) * gd)
        dg = wsum[:, cs] / float(wl) - u[:, cs]
        y = _pool_group_out(dg, g, wgrp_ref) * scale_ref[:, cs]
        xo_ref[:, cs] = x[:, cs] + y


def _pool_step(x, buf, nw, win, wgrp, scale, layer, o, *, pos0):
    assert pos0 + 1 >= max(POOL_WINDOWS)
    n, d = x.shape
    gd = d // len(POOL_WINDOWS)
    return pl.pallas_call(
        _pool_step_body,
        grid=(1,),
        in_specs=[
            _const_spec((n, d)),
            pl.BlockSpec((None, n, POOL_BUF, d), lambda i: (o, 0, 0, 0)),
            _layer_spec((1, d), layer),
            _layer_spec((d, d), o),
            _layer_spec((len(POOL_WINDOWS), gd, gd), o),
            _layer_spec((1, d), o),
        ],
        out_specs=[_const_spec((n, d)), _const_spec((n, POOL_BUF, d))],
        out_shape=[jax.ShapeDtypeStruct((n, d), F32), jax.ShapeDtypeStruct((n, POOL_BUF, d), F32)],
        scratch_shapes=[pltpu.VMEM((n, d), F32), pltpu.VMEM((n, d), F32)],
        compiler_params=pltpu.CompilerParams(
            dimension_semantics=("arbitrary",), vmem_limit_bytes=VMEM_LIMIT),
        name="pool_step",
    )(x, buf, nw, win, wgrp, scale)


def _pad_lanes(a, width=128):
    return jnp.pad(a, [(0, 0)] * (a.ndim - 1) + [(0, width - a.shape[-1])])


def kernel(x_prompt, x_sample, state_hgrn, state_gdn, state_gdn_conv, state_pool, hgrn_lb, w_in_even, hgrn_norm_w, gdn_conv_w, gdn_a_log, gdn_dt_bias, gdn_norm_w, w_out_even, pool_w_in, pool_w_group, pool_scale, norm_ffn1, ffn1_w_gate, ffn1_w_up, ffn1_w_down, norm_mix, norm_ffn2, ffn2_w_gate, ffn2_w_up, ffn2_w_down, final_norm):
    depth, d = norm_mix.shape
    bsz, seq, _ = x_prompt.shape
    n_s, dec_seq, _ = x_sample.shape
    assert dec_seq == 1
    mix = N_HEADS * HEAD_DIM
    n_main = 8 * mix
    past_len = 16384

    bf = lambda a: a.astype(BF16)
    win_main = bf(w_in_even[:, :, :n_main])
    win_ab = bf(_pad_lanes(w_in_even[:, :, n_main:]))
    wout = bf(w_out_even)
    pwin, pwgrp = bf(pool_w_in), bf(pool_w_group)
    f1 = (bf(ffn1_w_gate), bf(ffn1_w_up), bf(ffn1_w_down))
    f2 = (bf(ffn2_w_gate), bf(ffn2_w_up), bf(ffn2_w_down))
    lb_all = jnp.cumsum(jax.nn.softmax(hgrn_lb.astype(F32), axis=0), axis=0)
    lb_all = (lb_all - lb_all[:1])[:, None, :]
    row = lambda a: a[:, None, :]
    nf1, nmx, nf2 = row(norm_ffn1), row(norm_mix), row(norm_ffn2)
    hgw, gdw = row(hgrn_norm_w), row(gdn_norm_w)
    alog, dtb = row(_pad_lanes(gdn_a_log)), row(_pad_lanes(gdn_dt_bias))
    pscale = row(pool_scale)
    fnw = final_norm[None, :]

    xp = x_prompt
    xs = x_sample.reshape(n_s, d)
    hg_p, gd_p, cv_p, pl_p = [], [], [], []
    hg_s, gd_s, cv_s, pl_s = [], [], [], []
    for li in range(depth):
        last = li == depth - 1
        xp = _ffn(xp.reshape(bsz * seq, d), nf1, *f1, fnw, li, final=False, tm=512).reshape(bsz, seq, d)
        xs = _ffn(xs, nf1, *f1, fnw, li, final=False, tm=512)
        if li % 2 == 0:
            e = li // 2
            xp, a, b, c = _even_prompt(xp, nmx, win_main, win_ab, lb_all, hgw, gdn_conv_w, alog, dtb,
                                       gdw, wout, li, e, tb=256)
            hg_p.append(a), gd_p.append(b), cv_p.append(c)
            p, ab = _even_proj(xs, nmx, win_main, win_ab, li, e)
            o, a, b, c = _even_step(p, ab, state_hgrn, state_gdn, state_gdn_conv, lb_all, hgw,
                                    gdn_conv_w, alog, dtb, gdw, e, bb=8)
            xs = _out_proj(o, xs, wout, e)
            hg_s.append(a), gd_s.append(b), cv_s.append(c)
        else:
            o = li // 2
            xp, pb = _pool_prompt(xp, nmx, pwin, pwgrp, pscale, li, o, tb=512)
            pl_p.append(pb)
            xs, pb = _pool_step(xs, state_pool, nmx, pwin, pwgrp, pscale, li, o, pos0=past_len)
            pl_s.append(pb)
        xp = _ffn(xp.reshape(bsz * seq, d), nf2, *f2, fnw, li, final=last, tm=512).reshape(bsz, seq, d)
        xs = _ffn(xs, nf2, *f2, fnw, li, final=last, tm=512)

    st = jnp.stack
    return (xp, xs.reshape(n_s, 1, d), st(hg_p), st(gd_p), st(cv_p), st(pl_p),
            st(hg_s), st(gd_s), st(cv_s), st(pl_s))
```

```python
import functools

import jax
import jax.numpy as jnp
from jax import lax
from jax.experimental import pallas as pl
from jax.experimental.pallas import tpu as pltpu

F32 = jnp.float32
BF16 = jnp.bfloat16
EPS = 1e-6
HEAD_DIM = 128
N_HEADS = 4
GDN_CHUNK = 64
HG_BAND = 4
POOL_WINDOWS = (2, 4, 8, 16)
POOL_BUF = max(POOL_WINDOWS) - 1
GDN_CONV = 4
VMEM_LIMIT = 56 * 1024 * 1024
NEG_INF = float("-inf")


def _dot(a, b):
    return jnp.dot(a, b, preferred_element_type=F32)


def _dot_nt(a, b):
    return lax.dot_general(a, b, (((1,), (1,)), ((), ())), preferred_element_type=F32)


def _dot_tn(a, b):
    return lax.dot_general(a, b, (((0,), (0,)), ((), ())), preferred_element_type=F32)


def _masked_rowsum(mask16, x):
    hi = x.astype(BF16)
    r1 = x - hi.astype(F32)
    mid = r1.astype(BF16)
    lo = (r1 - mid.astype(F32)).astype(BF16)
    n = x.shape[1]
    y = _dot(mask16, jnp.concatenate([hi, mid, lo], axis=1))
    return y[:, :n] + y[:, n:2 * n] + y[:, 2 * n:]


def _rms(x, w):
    return x * lax.rsqrt(jnp.mean(x * x, axis=-1, keepdims=True) + EPS) * w


def _sigmoid(x):
    return 1.0 / (1.0 + jnp.exp(-x))


def _silu(x):
    return x * _sigmoid(x)


def _softplus(x):
    return jnp.maximum(x, 0.0) + jnp.log1p(jnp.exp(-jnp.abs(x)))


def _log_sigmoid(x):
    return jnp.minimum(x, 0.0) - jnp.log1p(jnp.exp(-jnp.abs(x)))


def _logaddexp(a, b):
    return jnp.maximum(a, b) + jnp.log1p(jnp.exp(-jnp.abs(a - b)))


def _head_rms(o, w):
    return o * lax.rsqrt(jnp.mean(o * o, axis=-1, keepdims=True) + EPS) * w


def _l2n(t):
    return t * lax.rsqrt(jnp.sum(t * t, axis=-1, keepdims=True) + EPS)


def _hgrn_gates(qa, fa, lb):
    logg = _logaddexp(jnp.log(lb), jnp.log1p(-lb) + _log_sigmoid(fa))
    k = (1.0 - lb) * _sigmoid(-fa)
    q = _silu(qa) * HEAD_DIM ** -0.5
    return q, k, logg


def _const_spec(shape):
    return pl.BlockSpec(shape, lambda *_: (0,) * len(shape))


def _layer_spec(shape, layer):
    return pl.BlockSpec((None,) + tuple(shape), lambda *_: (layer,) + (0,) * len(shape))


def _ffn_body(x_ref, nw_ref, wg_ref, wu_ref, wd_ref, fnw_ref, o_ref, acc_ref, *, ff_chunk, final):
    x = x_ref[...]
    h = _rms(x, nw_ref[...]).astype(BF16)
    d_ff = wg_ref.shape[1]
    for c in range(d_ff // ff_chunk):
        cs = slice(c * ff_chunk, (c + 1) * ff_chunk)
        g = _dot(h, wg_ref[:, cs])
        u = _dot(h, wu_ref[:, cs])
        a = (_silu(g) * u).astype(BF16)
        part = _dot(a, wd_ref[cs, :])
        if c == 0:
            acc_ref[...] = part
        else:
            acc_ref[...] += part
    y = x + 0.5 * acc_ref[...]
    if final:
        y = _rms(y, fnw_ref[...])
    o_ref[...] = y


def _ffn(x, nw, wg, wu, wd, fnw, layer, *, final, tm):
    n, d = x.shape
    d_ff = wg.shape[-1]
    tm = min(tm, n)
    body = functools.partial(_ffn_body, ff_chunk=256, final=final)
    return pl.pallas_call(
        body,
        grid=(n // tm,),
        in_specs=[
            pl.BlockSpec((tm, d), lambda i: (i, 0)),
            _layer_spec((1, d), layer),
            _layer_spec((d, d_ff), layer),
            _layer_spec((d, d_ff), layer),
            _layer_spec((d_ff, d), layer),
            _const_spec((1, d)),
        ],
        out_specs=pl.BlockSpec((tm, d), lambda i: (i, 0)),
        out_shape=jax.ShapeDtypeStruct((n, d), F32),
        scratch_shapes=[pltpu.VMEM((tm, d), F32)],
        compiler_params=pltpu.CompilerParams(
            dimension_semantics=("arbitrary",), vmem_limit_bytes=VMEM_LIMIT),
        name="ffn",
    )(x, nw, wg, wu, wd, fnw)


def _even_prompt_body(x_ref, nw_ref, win_ref, wab_ref, lb_ref, hgw_ref, convw_ref, alog_ref, dtb_ref,
                      gdw_ref, wout_ref,
                      xo_ref, shg_ref, sgd_ref, sconv_ref,
                      p_scr, ext_scr, o_scr, hgt_scr, vn_scr, *, tb):
    t = pl.program_id(1)
    nt = pl.num_programs(1)
    hd_w = HEAD_DIM
    mix = N_HEADS * hd_w
    tail = 8

    @pl.when(t == 0)
    def _init():
        hgt_scr[...] = jnp.zeros_like(hgt_scr)
        sgd_ref[...] = jnp.zeros_like(sgd_ref)
        ext_scr[0:tail, :] = jnp.zeros((tail, ext_scr.shape[1]), F32)

    x = x_ref[0]
    h = _rms(x, nw_ref[...]).astype(BF16)
    n_in = win_ref.shape[1]
    for c in range(n_in // 1024):
        cs = slice(c * 1024, (c + 1) * 1024)
        p_scr[:, cs] = _dot(h, win_ref[:, cs])
    ab = _dot(h, wab_ref[...])

    rows = lax.broadcasted_iota(jnp.int32, (tb, 1), 0)
    rr = lax.broadcasted_iota(jnp.int32, (tb, tb), 0)
    cc = lax.broadcasted_iota(jnp.int32, (tb, tb), 1)
    tril = (rr >= cc)

    tril16 = tril.astype(BF16)
    q, k, logg = _hgrn_gates(p_scr[:, 0:mix], p_scr[:, mix:2 * mix], lb_ref[...])
    v = p_scr[:, 2 * mix:3 * mix]
    b = _masked_rowsum(tril16, logg)
    v16 = v.astype(BF16)

    rb = rows & (HG_BAND - 1)
    zs = [q * k]
    vs = [v]
    for d in range(1, HG_BAND):
        e = jnp.where(rb >= d, b - pltpu.roll(b, d, 0), NEG_INF)
        zs.append(q * pltpu.roll(k, d, 0) * jnp.exp(e))
        vs.append(pltpu.roll(v, d, 0))

    levels = []
    rx = rr ^ cc
    lower = rr > cc
    m = HG_BAND
    while m < tb:
        g2 = 2 * m
        bm = jnp.broadcast_to(b.reshape(tb // g2, g2, mix)[:, m - 1:m, :],
                              (tb // g2, g2, mix)).reshape(tb, mix)
        second = (rows & (g2 - 1)) >= m
        y = (jnp.where(second, q, k) * jnp.exp(-jnp.abs(b - bm))).astype(BF16)
        keep = jnp.logical_and(lower, jnp.logical_and(rx >= m, rx < g2))
        levels.append((keep, y))
        m = g2
    qs = (q * jnp.exp(b)).astype(BF16)
    bl = b[tb - 1:tb, :]
    kdec = (k * jnp.exp(bl - b)).astype(BF16)
    ebl = jnp.exp(bl)

    for hd in range(N_HEADS):
        ls = slice(hd * hd_w, (hd + 1) * hd_w)
        o = jnp.sum(zs[0][:, ls], axis=-1, keepdims=True) * vs[0][:, ls]
        for d in range(1, HG_BAND):
            o = o + jnp.sum(zs[d][:, ls], axis=-1, keepdims=True) * vs[d][:, ls]
        att = None
        for keep, y in levels:
            am = jnp.where(keep, _dot_nt(y[:, ls], y[:, ls]), 0.0)
            att = am if att is None else att + am
        st = hgt_scr[hd]
        o = o + _dot(att.astype(BF16), v16[:, ls])
        o = o + _dot_nt(qs[:, ls], st.astype(BF16))
        hgt_scr[hd] = st * ebl[:, ls] + _dot_tn(v16[:, ls], kdec[:, ls])
        og = p_scr[:, 3 * mix + hd * hd_w: 3 * mix + (hd + 1) * hd_w]
        o = _head_rms(o, hgw_ref[...]) * _sigmoid(og)
        o_scr[:, ls] = o.astype(BF16)

    n_qkv = 3 * mix
    q0 = 4 * mix
    ext_scr[tail:tail + tb, :] = p_scr[:, q0:q0 + n_qkv]
    conv = ext_scr[tail:tail + tb, :] * convw_ref[GDN_CONV - 1:GDN_CONV, :]
    for j in range(1, GDN_CONV):
        conv = conv + ext_scr[tail - j:tail - j + tb, :] * convw_ref[GDN_CONV - 1 - j:GDN_CONV - j, :]
    keep = ext_scr[tb:tb + tail, :]
    ext_scr[0:tail, :] = keep
    conv = _silu(conv)

    g_all = -jnp.exp(alog_ref[...]) * _softplus(ab + dtb_ref[...])
    beta_all = _sigmoid(ab)
    csh = GDN_CHUNK.bit_length() - 1
    same = (rr >> csh) == (cc >> csh)
    incl = jnp.logical_and(same, tril)
    strict = jnp.logical_and(same, rr > cc)
    bcum = _masked_rowsum(incl.astype(BF16), g_all)
    bcum_t = bcum.T
    eye = (rr == cc).astype(F32)
    n_chunks = tb // GDN_CHUNK
    heads = range(N_HEADS)
    kh, kh16, vh, bcol, aqk, qe, tinv, ys, rhs = [], [], [], [], [], [], [], [], []
    for hd in heads:
        ls = slice(hd * hd_w, (hd + 1) * hd_w)
        qh = _l2n(conv[:, ls]) * hd_w ** -0.5
        kh.append(_l2n(conv[:, mix + hd * hd_w: mix + (hd + 1) * hd_w]))
        vh = conv[:, 2 * mix + hd * hd_w: 2 * mix + (hd + 1) * hd_w]
        bcol.append(bcum[:, hd:hd + 1])
        brow = bcum_t[hd:hd + 1, :]
        beta = beta_all[:, N_HEADS + hd:N_HEADS + hd + 1]
        decay = jnp.exp(jnp.where(incl, bcol[hd] - brow, NEG_INF))
        kb = kh[hd] * beta
        kh16.append(kh[hd].astype(BF16))
        gram = _dot_nt(kb.astype(BF16), kh16[hd])
        xm = -jnp.where(strict, gram * decay, 0.0)
        ys.append(xm)
        tinv.append(eye + xm)
        rhs.append(jnp.concatenate([vh * beta, kb * jnp.exp(bcol[hd])], axis=-1).astype(BF16))
        aqk.append((_dot_nt(qh.astype(BF16), kh16[hd]) * decay).astype(BF16))
        qe.append((qh * jnp.exp(bcol[hd])).astype(BF16))
    span = 2
    while span < GDN_CHUNK:
        for hd in heads:
            y16 = ys[hd].astype(BF16)
            ys[hd] = _dot(y16, y16)
        for hd in heads:
            tinv[hd] = tinv[hd] + _dot(tinv[hd].astype(BF16), ys[hd].astype(BF16))
        span *= 2
    us, ws = [], []
    for hd in heads:
        sol = _dot(tinv[hd].astype(BF16), rhs[hd])
        us.append(sol[:, :hd_w])
        ws.append(sol[:, hd_w:].astype(BF16))
    vn_scr[...] = jnp.zeros_like(vn_scr)
    for c in range(n_chunks):
        rs = slice(c * GDN_CHUNK, (c + 1) * GDN_CHUNK)
        for hd in heads:
            s = sgd_ref[0, hd]
            s16 = s.astype(BF16)
            vnew = us[hd][rs] - _dot(ws[hd][rs], s16)
            vn_scr[hd, rs, :] = vnew
            o_c = _dot(qe[hd][rs], s16) + _dot(aqk[hd][rs], vn_scr[hd].astype(BF16))
            bl = bcol[hd][(c + 1) * GDN_CHUNK - 1:(c + 1) * GDN_CHUNK, :]
            kdec = (kh[hd][rs] * jnp.exp(bl - bcol[hd][rs])).astype(BF16)
            sgd_ref[0, hd] = s * jnp.exp(bl) + _dot_tn(kdec, vnew.astype(BF16))
            z = p_scr[rs, q0 + n_qkv + hd * hd_w: q0 + n_qkv + (hd + 1) * hd_w]
            o_c = _head_rms(o_c, gdw_ref[...]) * _silu(z)
            o_scr[rs, mix + hd * hd_w: mix + (hd + 1) * hd_w] = o_c.astype(BF16)

    xo_ref[0] = x + _dot(o_scr[...], wout_ref[...])

    @pl.when(t == nt - 1)
    def _fin():
        for hd in range(N_HEADS):
            shg_ref[0, hd] = hgt_scr[hd].T
        sconv_ref[0] = ext_scr[tail - (GDN_CONV - 1):tail, :]


def _even_prompt(x, nw, win, wab, lb, hgw, convw, alog, dtb, gdw, wout, layer, e, *, tb):
    bsz, seq, d = x.shape
    n_in = win.shape[-1]
    mix = N_HEADS * HEAD_DIM
    n_qkv = 3 * mix
    tb = min(tb, seq)
    body = functools.partial(_even_prompt_body, tb=tb)
    st_shape = (bsz, N_HEADS, HEAD_DIM, HEAD_DIM)
    return pl.pallas_call(
        body,
        grid=(bsz, seq // tb),
        in_specs=[
            pl.BlockSpec((1, tb, d), lambda b, t: (b, t, 0)),
            _layer_spec((1, d), layer),
            _layer_spec((d, n_in), e),
            _layer_spec((d, 128), e),
            _layer_spec((1, mix), e),
            _layer_spec((1, HEAD_DIM), e),
            _layer_spec((GDN_CONV, n_qkv), e),
            _layer_spec((1, 128), e),
            _layer_spec((1, 128), e),
            _layer_spec((1, HEAD_DIM), e),
            _layer_spec((d, d), e),
        ],
        out_specs=[
            pl.BlockSpec((1, tb, d), lambda b, t: (b, t, 0)),
            pl.BlockSpec((1, N_HEADS, HEAD_DIM, HEAD_DIM), lambda b, t: (b, 0, 0, 0)),
            pl.BlockSpec((1, N_HEADS, HEAD_DIM, HEAD_DIM), lambda b, t: (b, 0, 0, 0)),
            pl.BlockSpec((1, GDN_CONV - 1, n_qkv), lambda b, t: (b, 0, 0)),
        ],
        out_shape=[
            jax.ShapeDtypeStruct((bsz, seq, d), F32),
            jax.ShapeDtypeStruct(st_shape, F32),
            jax.ShapeDtypeStruct(st_shape, F32),
            jax.ShapeDtypeStruct((bsz, GDN_CONV - 1, n_qkv), F32),
        ],
        scratch_shapes=[
            pltpu.VMEM((tb, n_in), F32),
            pltpu.VMEM((tb + 8, n_qkv), F32),
            pltpu.VMEM((tb, d), BF16),
            pltpu.VMEM((N_HEADS, HEAD_DIM, HEAD_DIM), F32),
            pltpu.VMEM((N_HEADS, tb, HEAD_DIM), F32),
        ],
        compiler_params=pltpu.CompilerParams(
            dimension_semantics=("arbitrary", "arbitrary"), vmem_limit_bytes=VMEM_LIMIT),
        name="even_prompt",
    )(x, nw, win, wab, lb, hgw, convw, alog, dtb, gdw, wout)


def _even_proj_body(x_ref, nw_ref, win_ref, wab_ref, p_ref, ab_ref):
    h = _rms(x_ref[...], nw_ref[...]).astype(BF16)
    n_in = win_ref.shape[1]
    for c in range(n_in // 1024):
        cs = slice(c * 1024, (c + 1) * 1024)
        p_ref[:, cs] = _dot(h, win_ref[:, cs])
    ab_ref[...] = _dot(h, wab_ref[...])


def _even_proj(x, nw, win, wab, layer, e):
    n, d = x.shape
    n_in = win.shape[-1]
    return pl.pallas_call(
        _even_proj_body,
        grid=(1,),
        in_specs=[
            _const_spec((n, d)),
            _layer_spec((1, d), layer),
            _layer_spec((d, n_in), e),
            _layer_spec((d, 128), e),
        ],
        out_specs=[_const_spec((n, n_in)), _const_spec((n, 128))],
        out_shape=[jax.ShapeDtypeStruct((n, n_in), F32), jax.ShapeDtypeStruct((n, 128), F32)],
        compiler_params=pltpu.CompilerParams(
            dimension_semantics=("arbitrary",), vmem_limit_bytes=VMEM_LIMIT),
        name="even_proj",
    )(x, nw, win, wab)


def _rows_to_cols(r):
    n = r.shape[0]
    if n < 128:
        r = jnp.concatenate([r, jnp.zeros((128 - n, r.shape[1]), r.dtype)], axis=0)
    return r.T


def _even_step_body(p_ref, ab_ref, shg_ref, sgd_ref, sconv_ref, lb_ref, hgw_ref, convw_ref, alog_ref,
                    dtb_ref, gdw_ref, *rest, bb):
    o_ref, shg_o, sgd_o, sconv_o, oa_scr, ob_scr = rest[-6:]
    hd_w = HEAD_DIM
    mix = N_HEADS * hd_w
    n_qkv = 3 * mix
    q0 = 4 * mix
    ab = ab_ref[...]
    g_all = -jnp.exp(alog_ref[...]) * _softplus(ab + dtb_ref[...])
    eg_all = jnp.exp(g_all)
    beta_all = _sigmoid(ab)

    qkv = p_ref[:, q0:q0 + n_qkv]
    conv = qkv * convw_ref[GDN_CONV - 1:GDN_CONV, :]
    for j in range(GDN_CONV - 1):
        conv = conv + sconv_ref[:, j, :] * convw_ref[j:j + 1, :]
    for j in range(GDN_CONV - 2):
        sconv_o[:, j, :] = sconv_ref[:, j + 1, :]
    sconv_o[:, GDN_CONV - 2, :] = qkv
    conv = _silu(conv)

    for hd in range(N_HEADS):
        ls = slice(hd * hd_w, (hd + 1) * hd_w)
        q, k, logg = _hgrn_gates(p_ref[:, ls], p_ref[:, mix + hd * hd_w: mix + (hd + 1) * hd_w],
                                 lb_ref[:, ls])
        v = p_ref[:, 2 * mix + hd * hd_w: 2 * mix + (hd + 1) * hd_w]
        qc = _rows_to_cols(q)
        kc = _rows_to_cols(k)
        gc = _rows_to_cols(jnp.exp(logg))
        q2 = _l2n(conv[:, ls]) * hd_w ** -0.5
        k2 = _l2n(conv[:, mix + hd * hd_w: mix + (hd + 1) * hd_w])
        v2 = conv[:, 2 * mix + hd * hd_w: 2 * mix + (hd + 1) * hd_w]
        q2c = _rows_to_cols(q2)
        k2c = _rows_to_cols(k2)
        for j in range(bb):
            s = shg_ref[j, hd]
            s_new = s * gc[:, j:j + 1] + kc[:, j:j + 1] * v[j:j + 1, :]
            shg_o[j, hd] = s_new
            oa_scr[j:j + 1, ls] = jnp.sum(s_new * qc[:, j:j + 1], axis=0, keepdims=True)

            s2 = sgd_ref[j, hd] * eg_all[j:j + 1, hd:hd + 1]
            ks = jnp.sum(s2 * k2c[:, j:j + 1], axis=0, keepdims=True)
            vnew = beta_all[j:j + 1, N_HEADS + hd:N_HEADS + hd + 1] * (v2[j:j + 1, :] - ks)
            s2_new = s2 + k2c[:, j:j + 1] * vnew
            sgd_o[j, hd] = s2_new
            ob_scr[j:j + 1, ls] = jnp.sum(s2_new * q2c[:, j:j + 1], axis=0, keepdims=True)

    for hd in range(N_HEADS):
        ls = slice(hd * hd_w, (hd + 1) * hd_w)
        og = p_ref[:, 3 * mix + hd * hd_w: 3 * mix + (hd + 1) * hd_w]
        z = p_ref[:, q0 + n_qkv + hd * hd_w: q0 + n_qkv + (hd + 1) * hd_w]
        o_ref[:, ls] = (_head_rms(oa_scr[:, ls], hgw_ref[...]) * _sigmoid(og)).astype(BF16)
        o_ref[:, mix + hd * hd_w: mix + (hd + 1) * hd_w] = (
            _head_rms(ob_scr[:, ls], gdw_ref[...]) * _silu(z)).astype(BF16)


def _even_step(p, ab, shg, sgd, sconv, lb, hgw, convw, alog, dtb, gdw, e, prev, *, bb):
    n, n_in = p.shape
    mix = N_HEADS * HEAD_DIM
    n_qkv = 3 * mix
    d = 2 * mix
    body = functools.partial(_even_step_body, bb=bb)
    st_block = (None, bb, N_HEADS, HEAD_DIM, HEAD_DIM)
    st_spec = pl.BlockSpec(st_block, lambda i: (e, i, 0, 0, 0))
    cv_spec = pl.BlockSpec((None, bb, GDN_CONV - 1, n_qkv), lambda i: (e, i, 0, 0))
    n_fixed = 11
    prev = tuple(prev)
    return pl.pallas_call(
        body,
        grid=(n // bb,),
        in_specs=[
            pl.BlockSpec((bb, n_in), lambda i: (i, 0)),
            pl.BlockSpec((bb, 128), lambda i: (i, 0)),
            st_spec, st_spec, cv_spec,
            _layer_spec((1, mix), e),
            _layer_spec((1, HEAD_DIM), e),
            _layer_spec((GDN_CONV, n_qkv), e),
            _layer_spec((1, 128), e),
            _layer_spec((1, 128), e),
            _layer_spec((1, HEAD_DIM), e),
        ] + [pl.BlockSpec(memory_space=pl.ANY)] * len(prev),
        out_specs=[pl.BlockSpec((bb, d), lambda i: (i, 0)), st_spec, st_spec, cv_spec],
        out_shape=[
            jax.ShapeDtypeStruct((n, d), BF16),
            jax.ShapeDtypeStruct(shg.shape, F32),
            jax.ShapeDtypeStruct(sgd.shape, F32),
            jax.ShapeDtypeStruct(sconv.shape, F32),
        ],
        input_output_aliases={n_fixed + j: 1 + j for j in range(len(prev))},
        scratch_shapes=[pltpu.VMEM((bb, mix), F32), pltpu.VMEM((bb, mix), F32)],
        compiler_params=pltpu.CompilerParams(
            dimension_semantics=("arbitrary",), vmem_limit_bytes=VMEM_LIMIT),
        name="even_step",
    )(p, ab, shg, sgd, sconv, lb, hgw, convw, alog, dtb, gdw, *prev)


def _out_proj_body(o_ref, x_ref, w_ref, xo_ref):
    xo_ref[...] = x_ref[...] + _dot(o_ref[...], w_ref[...])


def _out_proj(o, x, wout, e):
    n, d = x.shape
    return pl.pallas_call(
        _out_proj_body,
        grid=(1,),
        in_specs=[_const_spec((n, d)), _const_spec((n, d)), _layer_spec((d, d), e)],
        out_specs=_const_spec((n, d)),
        out_shape=jax.ShapeDtypeStruct((n, d), F32),
        compiler_params=pltpu.CompilerParams(
            dimension_semantics=("arbitrary",), vmem_limit_bytes=VMEM_LIMIT),
        name="out_proj",
    )(o, x, wout)


def _pool_group_out(d_grp, g, wgrp_ref):
    return _dot(d_grp.astype(BF16), wgrp_ref[g])


def _pool_prompt_body(x_ref, nw_ref, win_ref, wgrp_ref, scale_ref, xo_ref, buf_ref, ext_scr, *, tb):
    t = pl.program_id(1)
    nt = pl.num_programs(1)
    carry = 16
    gd = ext_scr.shape[1] // len(POOL_WINDOWS)

    @pl.when(t == 0)
    def _init():
        ext_scr[0:carry, :] = jnp.zeros((carry, ext_scr.shape[1]), F32)

    x = x_ref[0]
    h = _rms(x, nw_ref[...]).astype(BF16)
    u = _dot(h, win_ref[...])
    ext_scr[carry:carry + tb, :] = u
    pos = t * tb + lax.broadcasted_iota(jnp.int32, (tb, 1), 0)
    for g, wlen in enumerate(POOL_WINDOWS):
        cs = slice(g * gd, (g + 1) * gd)
        s = ext_scr[:, cs]
        span = 1
        while span < wlen:
            s = s + pltpu.roll(s, span, 0)
            span *= 2
        cnt = jnp.minimum(pos + 1, wlen).astype(F32)
        dg = s[carry:carry + tb, :] / cnt - u[:, cs]
        y = _pool_group_out(dg, g, wgrp_ref) * scale_ref[:, cs]
        xo_ref[0, :, cs] = x[:, cs] + y
    keep = ext_scr[tb:tb + carry, :]
    ext_scr[0:carry, :] = keep

    @pl.when(t == nt - 1)
    def _fin():
        buf_ref[0] = ext_scr[carry - POOL_BUF:carry, :]


def _pool_prompt(x, nw, win, wgrp, scale, layer, o, *, tb):
    bsz, seq, d = x.shape
    gd = d // len(POOL_WINDOWS)
    tb = min(tb, seq)
    body = functools.partial(_pool_prompt_body, tb=tb)
    return pl.pallas_call(
        body,
        grid=(bsz, seq // tb),
        in_specs=[
            pl.BlockSpec((1, tb, d), lambda b, t: (b, t, 0)),
            _layer_spec((1, d), layer),
            _layer_spec((d, d), o),
            _layer_spec((len(POOL_WINDOWS), gd, gd), o),
            _layer_spec((1, d), o),
        ],
        out_specs=[
            pl.BlockSpec((1, tb, d), lambda b, t: (b, t, 0)),
            pl.BlockSpec((1, POOL_BUF, d), lambda b, t: (b, 0, 0)),
        ],
        out_shape=[
            jax.ShapeDtypeStruct((bsz, seq, d), F32),
            jax.ShapeDtypeStruct((bsz, POOL_BUF, d), F32),
        ],
        scratch_shapes=[pltpu.VMEM((tb + 16, d), F32)],
        compiler_params=pltpu.CompilerParams(
            dimension_semantics=("arbitrary", "arbitrary"), vmem_limit_bytes=VMEM_LIMIT),
        name="pool_prompt",
    )(x, nw, win, wgrp, scale)


def _pool_step_body(x_ref, buf_ref, nw_ref, win_ref, wgrp_ref, scale_ref, *rest):
    xo_ref, buf_o, u_scr, win_scr = rest[-4:]
    n, d = x_ref.shape
    gd = d // len(POOL_WINDOWS)
    x = x_ref[...]
    h = _rms(x, nw_ref[...]).astype(BF16)
    u = _dot(h, win_ref[...])
    u_scr[...] = u
    lane_grp = lax.broadcasted_iota(jnp.int32, (POOL_BUF, d), 1) // gd
    rowi = lax.broadcasted_iota(jnp.int32, (POOL_BUF, d), 0)
    wlen = jnp.zeros((POOL_BUF, d), jnp.int32)
    for g, wl in enumerate(POOL_WINDOWS):
        wlen = jnp.where(lane_grp == g, wl, wlen)
    in_win = rowi >= (POOL_BUF + 1 - wlen)

    def per_seq(b, carry):
        rows = buf_ref[b]
        win_scr[pl.ds(b, 1), :] = jnp.sum(jnp.where(in_win, rows, 0.0), axis=0, keepdims=True)
        buf_o[b, 0:POOL_BUF - 1, :] = buf_ref[b, 1:POOL_BUF, :]
        buf_o[b, POOL_BUF - 1:POOL_BUF, :] = u_scr[pl.ds(b, 1), :]
        return carry

    lax.fori_loop(0, n, per_seq, 0)
    wsum = win_scr[...] + u
    for g, wl in enumerate(POOL_WINDOWS):
        cs = slice(g * gd, (g + 1) * gd)
        dg = wsum[:, cs] / float(wl) - u[:, cs]
        y = _pool_group_out(dg, g, wgrp_ref) * scale_ref[:, cs]
        xo_ref[:, cs] = x[:, cs] + y


def _pool_step(x, buf, nw, win, wgrp, scale, layer, o, prev, *, pos0):
    assert pos0 + 1 >= max(POOL_WINDOWS)
    n, d = x.shape
    gd = d // len(POOL_WINDOWS)
    return pl.pallas_call(
        _pool_step_body,
        grid=(1,),
        in_specs=[
            _const_spec((n, d)),
            pl.BlockSpec((None, n, POOL_BUF, d), lambda i: (o, 0, 0, 0)),
            _layer_spec((1, d), layer),
            _layer_spec((d, d), o),
            _layer_spec((len(POOL_WINDOWS), gd, gd), o),
            _layer_spec((1, d), o),
        ] + [pl.BlockSpec(memory_space=pl.ANY)] * len(prev),
        out_specs=[_const_spec((n, d)),
                   pl.BlockSpec((None, n, POOL_BUF, d), lambda i: (o, 0, 0, 0))],
        out_shape=[jax.ShapeDtypeStruct((n, d), F32), jax.ShapeDtypeStruct(buf.shape, F32)],
        input_output_aliases={6 + j: 1 + j for j in range(len(prev))},
        scratch_shapes=[pltpu.VMEM((n, d), F32), pltpu.VMEM((n, d), F32)],
        compiler_params=pltpu.CompilerParams(
            dimension_semantics=("arbitrary",), vmem_limit_bytes=VMEM_LIMIT),
        name="pool_step",
    )(x, buf, nw, win, wgrp, scale, *prev)


def _pad_lanes(a, width=128):
    return jnp.pad(a, [(0, 0)] * (a.ndim - 1) + [(0, width - a.shape[-1])])


def kernel(x_prompt, x_sample, state_hgrn, state_gdn, state_gdn_conv, state_pool, hgrn_lb, w_in_even, hgrn_norm_w, gdn_conv_w, gdn_a_log, gdn_dt_bias, gdn_norm_w, w_out_even, pool_w_in, pool_w_group, pool_scale, norm_ffn1, ffn1_w_gate, ffn1_w_up, ffn1_w_down, norm_mix, norm_ffn2, ffn2_w_gate, ffn2_w_up, ffn2_w_down, final_norm):
    depth, d = norm_mix.shape
    bsz, seq, _ = x_prompt.shape
    n_s, dec_seq, _ = x_sample.shape
    assert dec_seq == 1
    mix = N_HEADS * HEAD_DIM
    n_main = 8 * mix
    past_len = 16384

    bf = lambda a: a.astype(BF16)
    win_main = bf(w_in_even[:, :, :n_main])
    win_ab = bf(_pad_lanes(w_in_even[:, :, n_main:]))
    wout = bf(w_out_even)
    pwin, pwgrp = bf(pool_w_in), bf(pool_w_group)
    f1 = (bf(ffn1_w_gate), bf(ffn1_w_up), bf(ffn1_w_down))
    f2 = (bf(ffn2_w_gate), bf(ffn2_w_up), bf(ffn2_w_down))
    lb_all = jnp.cumsum(jax.nn.softmax(hgrn_lb.astype(F32), axis=0), axis=0)
    lb_all = (lb_all - lb_all[:1])[:, None, :]
    row = lambda a: a[:, None, :]
    nf1, nmx, nf2 = row(norm_ffn1), row(norm_mix), row(norm_ffn2)
    hgw, gdw = row(hgrn_norm_w), row(gdn_norm_w)
    alog, dtb = row(_pad_lanes(gdn_a_log)), row(_pad_lanes(gdn_dt_bias))
    pscale = row(pool_scale)
    fnw = final_norm[None, :]

    xp = x_prompt
    xs = x_sample.reshape(n_s, d)
    hg_p, gd_p, cv_p, pl_p = [], [], [], []
    even_s, pool_s = (), ()
    for li in range(depth):
        last = li == depth - 1
        xp = _ffn(xp.reshape(bsz * seq, d), nf1, *f1, fnw, li, final=False, tm=512).reshape(bsz, seq, d)
        xs = _ffn(xs, nf1, *f1, fnw, li, final=False, tm=512)
        if li % 2 == 0:
            e = li // 2
            xp, a, b, c = _even_prompt(xp, nmx, win_main, win_ab, lb_all, hgw, gdn_conv_w, alog, dtb,
                                       gdw, wout, li, e, tb=256)
            hg_p.append(a), gd_p.append(b), cv_p.append(c)
            p, ab = _even_proj(xs, nmx, win_main, win_ab, li, e)
            o, *even_s = _even_step(p, ab, state_hgrn, state_gdn, state_gdn_conv, lb_all, hgw,
                                    gdn_conv_w, alog, dtb, gdw, e, even_s, bb=8)
            xs = _out_proj(o, xs, wout, e)
        else:
            o = li // 2
            xp, pb = _pool_prompt(xp, nmx, pwin, pwgrp, pscale, li, o, tb=512)
            pl_p.append(pb)
            xs, *pool_s = _pool_step(xs, state_pool, nmx, pwin, pwgrp, pscale, li, o, pool_s,
                                     pos0=past_len)
        xp = _ffn(xp.reshape(bsz * seq, d), nf2, *f2, fnw, li, final=last, tm=512).reshape(bsz, seq, d)
        xs = _ffn(xs, nf2, *f2, fnw, li, final=last, tm=512)

    st = jnp.stack
    return (xp, xs.reshape(n_s, 1, d), st(hg_p), st(gd_p), st(cv_p), st(pl_p), *even_s, *pool_s)
```

```python
import functools

import jax
import jax.numpy as jnp
from jax import lax
from jax.experimental import pallas as pl
from jax.experimental.pallas import tpu as pltpu

F32 = jnp.float32
BF16 = jnp.bfloat16
EPS = 1e-6
HEAD_DIM = 128
N_HEADS = 4
GDN_CHUNK = 64
HG_BAND = 4
POOL_WINDOWS = (2, 4, 8, 16)
POOL_BUF = max(POOL_WINDOWS) - 1
GDN_CONV = 4
VMEM_LIMIT = 56 * 1024 * 1024
NEG_INF = float("-inf")


def _dot(a, b):
    return jnp.dot(a, b, preferred_element_type=F32)


def _dot_nt(a, b):
    return lax.dot_general(a, b, (((1,), (1,)), ((), ())), preferred_element_type=F32)


def _dot_tn(a, b):
    return lax.dot_general(a, b, (((0,), (0,)), ((), ())), preferred_element_type=F32)


def _masked_rowsum(mask16, x):
    hi = x.astype(BF16)
    r1 = x - hi.astype(F32)
    mid = r1.astype(BF16)
    lo = (r1 - mid.astype(F32)).astype(BF16)
    n = x.shape[1]
    y = _dot(mask16, jnp.concatenate([hi, mid, lo], axis=1))
    return y[:, :n] + y[:, n:2 * n] + y[:, 2 * n:]


def _rms(x, w):
    return x * lax.rsqrt(jnp.mean(x * x, axis=-1, keepdims=True) + EPS) * w


def _sigmoid(x):
    return 1.0 / (1.0 + jnp.exp(-x))


def _silu(x):
    return x * _sigmoid(x)


def _softplus(x):
    return jnp.maximum(x, 0.0) + jnp.log1p(jnp.exp(-jnp.abs(x)))


def _log_sigmoid(x):
    return jnp.minimum(x, 0.0) - jnp.log1p(jnp.exp(-jnp.abs(x)))


def _logaddexp(a, b):
    return jnp.maximum(a, b) + jnp.log1p(jnp.exp(-jnp.abs(a - b)))


def _head_rms(o, w):
    return o * lax.rsqrt(jnp.mean(o * o, axis=-1, keepdims=True) + EPS) * w


def _l2n(t):
    return t * lax.rsqrt(jnp.sum(t * t, axis=-1, keepdims=True) + EPS)


def _hgrn_gates(qa, fa, lb):
    logg = _logaddexp(jnp.log(lb), jnp.log1p(-lb) + _log_sigmoid(fa))
    k = (1.0 - lb) * _sigmoid(-fa)
    q = _silu(qa) * HEAD_DIM ** -0.5
    return q, k, logg


def _const_spec(shape):
    return pl.BlockSpec(shape, lambda *_: (0,) * len(shape))


def _layer_spec(shape, layer):
    return pl.BlockSpec((None,) + tuple(shape), lambda *_: (layer,) + (0,) * len(shape))


def _ffn_body(x_ref, nw_ref, wg_ref, wu_ref, wd_ref, fnw_ref, *rest, ff_chunk, final, pre):
    o_ref, acc_ref = rest[-2:]
    x = x_ref[...]
    if pre:
        mix_ref, wout_ref = rest[:2]
        x = x + _dot(mix_ref[...], wout_ref[...])
    h = _rms(x, nw_ref[...]).astype(BF16)
    d_ff = wg_ref.shape[1]
    for c in range(d_ff // ff_chunk):
        cs = slice(c * ff_chunk, (c + 1) * ff_chunk)
        g = _dot(h, wg_ref[:, cs])
        u = _dot(h, wu_ref[:, cs])
        a = (_silu(g) * u).astype(BF16)
        part = _dot(a, wd_ref[cs, :])
        if c == 0:
            acc_ref[...] = part
        else:
            acc_ref[...] += part
    y = x + 0.5 * acc_ref[...]
    if final:
        y = _rms(y, fnw_ref[...])
    o_ref[...] = y


def _ffn(x, nw, wg, wu, wd, fnw, layer, *, final, tm, pre=None):
    n, d = x.shape
    d_ff = wg.shape[-1]
    tm = min(tm, n)
    body = functools.partial(_ffn_body, ff_chunk=256, final=final, pre=pre is not None)
    extra_specs, extra_args = [], []
    if pre is not None:
        extra_specs = [pl.BlockSpec((tm, d), lambda i: (i, 0)), _layer_spec((d, d), pre[2])]
        extra_args = [pre[0], pre[1]]
    return pl.pallas_call(
        body,
        grid=(n // tm,),
        in_specs=[
            pl.BlockSpec((tm, d), lambda i: (i, 0)),
            _layer_spec((1, d), layer),
            _layer_spec((d, d_ff), layer),
            _layer_spec((d, d_ff), layer),
            _layer_spec((d_ff, d), layer),
            _const_spec((1, d)),
        ] + extra_specs,
        out_specs=pl.BlockSpec((tm, d), lambda i: (i, 0)),
        out_shape=jax.ShapeDtypeStruct((n, d), F32),
        scratch_shapes=[pltpu.VMEM((tm, d), F32)],
        compiler_params=pltpu.CompilerParams(
            dimension_semantics=("arbitrary",), vmem_limit_bytes=VMEM_LIMIT),
        name="ffn",
    )(x, nw, wg, wu, wd, fnw, *extra_args)


def _even_prompt_body(x_ref, nw_ref, win_ref, wab_ref, lb_ref, hgw_ref, convw_ref, alog_ref, dtb_ref,
                      gdw_ref, wout_ref,
                      xo_ref, shg_ref, sgd_ref, sconv_ref,
                      p_scr, ext_scr, o_scr, hgt_scr, vn_scr, *, tb):
    t = pl.program_id(1)
    nt = pl.num_programs(1)
    hd_w = HEAD_DIM
    mix = N_HEADS * hd_w
    tail = 8

    @pl.when(t == 0)
    def _init():
        hgt_scr[...] = jnp.zeros_like(hgt_scr)
        sgd_ref[...] = jnp.zeros_like(sgd_ref)
        ext_scr[0:tail, :] = jnp.zeros((tail, ext_scr.shape[1]), F32)

    x = x_ref[0]
    h = _rms(x, nw_ref[...]).astype(BF16)
    n_in = win_ref.shape[1]
    for c in range(n_in // 1024):
        cs = slice(c * 1024, (c + 1) * 1024)
        p_scr[:, cs] = _dot(h, win_ref[:, cs])
    ab = _dot(h, wab_ref[...])

    rows = lax.broadcasted_iota(jnp.int32, (tb, 1), 0)
    rr = lax.broadcasted_iota(jnp.int32, (tb, tb), 0)
    cc = lax.broadcasted_iota(jnp.int32, (tb, tb), 1)
    tril = (rr >= cc)

    tril16 = tril.astype(BF16)
    q, k, logg = _hgrn_gates(p_scr[:, 0:mix], p_scr[:, mix:2 * mix], lb_ref[...])
    v = p_scr[:, 2 * mix:3 * mix]
    b = _masked_rowsum(tril16, logg)
    v16 = v.astype(BF16)

    rb = rows & (HG_BAND - 1)
    zs = [q * k]
    vs = [v]
    for d in range(1, HG_BAND):
        e = jnp.where(rb >= d, b - pltpu.roll(b, d, 0), NEG_INF)
        zs.append(q * pltpu.roll(k, d, 0) * jnp.exp(e))
        vs.append(pltpu.roll(v, d, 0))

    levels = []
    rx = rr ^ cc
    lower = rr > cc
    m = HG_BAND
    while m < tb:
        g2 = 2 * m
        bm = jnp.broadcast_to(b.reshape(tb // g2, g2, mix)[:, m - 1:m, :],
                              (tb // g2, g2, mix)).reshape(tb, mix)
        second = (rows & (g2 - 1)) >= m
        y = (jnp.where(second, q, k) * jnp.exp(-jnp.abs(b - bm))).astype(BF16)
        keep = jnp.logical_and(lower, jnp.logical_and(rx >= m, rx < g2))
        levels.append((keep, y))
        m = g2
    qs = (q * jnp.exp(b)).astype(BF16)
    bl = b[tb - 1:tb, :]
    kdec = (k * jnp.exp(bl - b)).astype(BF16)
    ebl = jnp.exp(bl)

    for hd in range(N_HEADS):
        ls = slice(hd * hd_w, (hd + 1) * hd_w)
        o = jnp.sum(zs[0][:, ls], axis=-1, keepdims=True) * vs[0][:, ls]
        for d in range(1, HG_BAND):
            o = o + jnp.sum(zs[d][:, ls], axis=-1, keepdims=True) * vs[d][:, ls]
        att = None
        for keep, y in levels:
            am = jnp.where(keep, _dot_nt(y[:, ls], y[:, ls]), 0.0)
            att = am if att is None else att + am
        st = hgt_scr[hd]
        o = o + _dot(att.astype(BF16), v16[:, ls])
        o = o + _dot_nt(qs[:, ls], st.astype(BF16))
        hgt_scr[hd] = st * ebl[:, ls] + _dot_tn(v16[:, ls], kdec[:, ls])
        og = p_scr[:, 3 * mix + hd * hd_w: 3 * mix + (hd + 1) * hd_w]
        o = _head_rms(o, hgw_ref[...]) * _sigmoid(og)
        o_scr[:, ls] = o.astype(BF16)

    n_qkv = 3 * mix
    q0 = 4 * mix
    ext_scr[tail:tail + tb, :] = p_scr[:, q0:q0 + n_qkv]
    conv = ext_scr[tail:tail + tb, :] * convw_ref[GDN_CONV - 1:GDN_CONV, :]
    for j in range(1, GDN_CONV):
        conv = conv + ext_scr[tail - j:tail - j + tb, :] * convw_ref[GDN_CONV - 1 - j:GDN_CONV - j, :]
    keep = ext_scr[tb:tb + tail, :]
    ext_scr[0:tail, :] = keep
    conv = _silu(conv)

    g_all = -jnp.exp(alog_ref[...]) * _softplus(ab + dtb_ref[...])
    beta_all = _sigmoid(ab)
    csh = GDN_CHUNK.bit_length() - 1
    same = (rr >> csh) == (cc >> csh)
    incl = jnp.logical_and(same, tril)
    strict = jnp.logical_and(same, rr > cc)
    bcum = _masked_rowsum(incl.astype(BF16), g_all)
    bcum_t = bcum.T
    eye = (rr == cc).astype(F32)
    n_chunks = tb // GDN_CHUNK
    heads = range(N_HEADS)
    kh, kh16, vh, bcol, aqk, qe, tinv, ys, rhs = [], [], [], [], [], [], [], [], []
    for hd in heads:
        ls = slice(hd * hd_w, (hd + 1) * hd_w)
        qh = _l2n(conv[:, ls]) * hd_w ** -0.5
        kh.append(_l2n(conv[:, mix + hd * hd_w: mix + (hd + 1) * hd_w]))
        vh = conv[:, 2 * mix + hd * hd_w: 2 * mix + (hd + 1) * hd_w]
        bcol.append(bcum[:, hd:hd + 1])
        brow = bcum_t[hd:hd + 1, :]
        beta = beta_all[:, N_HEADS + hd:N_HEADS + hd + 1]
        decay = jnp.exp(jnp.where(incl, bcol[hd] - brow, NEG_INF))
        kb = kh[hd] * beta
        kh16.append(kh[hd].astype(BF16))
        gram = _dot_nt(kb.astype(BF16), kh16[hd])
        xm = -jnp.where(strict, gram * decay, 0.0)
        ys.append(xm)
        tinv.append(eye + xm)
        rhs.append(jnp.concatenate([vh * beta, kb * jnp.exp(bcol[hd])], axis=-1).astype(BF16))
        aqk.append((_dot_nt(qh.astype(BF16), kh16[hd]) * decay).astype(BF16))
        qe.append((qh * jnp.exp(bcol[hd])).astype(BF16))
    span = 2
    while span < GDN_CHUNK:
        for hd in heads:
            y16 = ys[hd].astype(BF16)
            ys[hd] = _dot(y16, y16)
        for hd in heads:
            tinv[hd] = tinv[hd] + _dot(tinv[hd].astype(BF16), ys[hd].astype(BF16))
        span *= 2
    us, ws = [], []
    for hd in heads:
        sol = _dot(tinv[hd].astype(BF16), rhs[hd])
        us.append(sol[:, :hd_w])
        ws.append(sol[:, hd_w:].astype(BF16))
    vn_scr[...] = jnp.zeros_like(vn_scr)
    for c in range(n_chunks):
        rs = slice(c * GDN_CHUNK, (c + 1) * GDN_CHUNK)
        for hd in heads:
            s = sgd_ref[0, hd]
            s16 = s.astype(BF16)
            vnew = us[hd][rs] - _dot(ws[hd][rs], s16)
            vn_scr[hd, rs, :] = vnew
            o_c = _dot(qe[hd][rs], s16) + _dot(aqk[hd][rs], vn_scr[hd].astype(BF16))
            bl = bcol[hd][(c + 1) * GDN_CHUNK - 1:(c + 1) * GDN_CHUNK, :]
            kdec = (kh[hd][rs] * jnp.exp(bl - bcol[hd][rs])).astype(BF16)
            sgd_ref[0, hd] = s * jnp.exp(bl) + _dot_tn(kdec, vnew.astype(BF16))
            z = p_scr[rs, q0 + n_qkv + hd * hd_w: q0 + n_qkv + (hd + 1) * hd_w]
            o_c = _head_rms(o_c, gdw_ref[...]) * _silu(z)
            o_scr[rs, mix + hd * hd_w: mix + (hd + 1) * hd_w] = o_c.astype(BF16)

    xo_ref[0] = x + _dot(o_scr[...], wout_ref[...])

    @pl.when(t == nt - 1)
    def _fin():
        for hd in range(N_HEADS):
            shg_ref[0, hd] = hgt_scr[hd].T
        sconv_ref[0] = ext_scr[tail - (GDN_CONV - 1):tail, :]


def _even_prompt(x, nw, win, wab, lb, hgw, convw, alog, dtb, gdw, wout, layer, e, *, tb):
    bsz, seq, d = x.shape
    n_in = win.shape[-1]
    mix = N_HEADS * HEAD_DIM
    n_qkv = 3 * mix
    tb = min(tb, seq)
    body = functools.partial(_even_prompt_body, tb=tb)
    st_shape = (bsz, N_HEADS, HEAD_DIM, HEAD_DIM)
    return pl.pallas_call(
        body,
        grid=(bsz, seq // tb),
        in_specs=[
            pl.BlockSpec((1, tb, d), lambda b, t: (b, t, 0)),
            _layer_spec((1, d), layer),
            _layer_spec((d, n_in), e),
            _layer_spec((d, 128), e),
            _layer_spec((1, mix), e),
            _layer_spec((1, HEAD_DIM), e),
            _layer_spec((GDN_CONV, n_qkv), e),
            _layer_spec((1, 128), e),
            _layer_spec((1, 128), e),
            _layer_spec((1, HEAD_DIM), e),
            _layer_spec((d, d), e),
        ],
        out_specs=[
            pl.BlockSpec((1, tb, d), lambda b, t: (b, t, 0)),
            pl.BlockSpec((1, N_HEADS, HEAD_DIM, HEAD_DIM), lambda b, t: (b, 0, 0, 0)),
            pl.BlockSpec((1, N_HEADS, HEAD_DIM, HEAD_DIM), lambda b, t: (b, 0, 0, 0)),
            pl.BlockSpec((1, GDN_CONV - 1, n_qkv), lambda b, t: (b, 0, 0)),
        ],
        out_shape=[
            jax.ShapeDtypeStruct((bsz, seq, d), F32),
            jax.ShapeDtypeStruct(st_shape, F32),
            jax.ShapeDtypeStruct(st_shape, F32),
            jax.ShapeDtypeStruct((bsz, GDN_CONV - 1, n_qkv), F32),
        ],
        scratch_shapes=[
            pltpu.VMEM((tb, n_in), F32),
            pltpu.VMEM((tb + 8, n_qkv), F32),
            pltpu.VMEM((tb, d), BF16),
            pltpu.VMEM((N_HEADS, HEAD_DIM, HEAD_DIM), F32),
            pltpu.VMEM((N_HEADS, tb, HEAD_DIM), F32),
        ],
        compiler_params=pltpu.CompilerParams(
            dimension_semantics=("arbitrary", "arbitrary"), vmem_limit_bytes=VMEM_LIMIT),
        name="even_prompt",
    )(x, nw, win, wab, lb, hgw, convw, alog, dtb, gdw, wout)


def _even_proj_body(x_ref, nw_ref, win_ref, wab_ref, p_ref, ab_ref):
    h = _rms(x_ref[...], nw_ref[...]).astype(BF16)
    n_in = win_ref.shape[1]
    for c in range(n_in // 1024):
        cs = slice(c * 1024, (c + 1) * 1024)
        p_ref[:, cs] = _dot(h, win_ref[:, cs])
    ab_ref[...] = _dot(h, wab_ref[...])


def _even_proj(x, nw, win, wab, layer, e):
    n, d = x.shape
    n_in = win.shape[-1]
    return pl.pallas_call(
        _even_proj_body,
        grid=(1,),
        in_specs=[
            _const_spec((n, d)),
            _layer_spec((1, d), layer),
            _layer_spec((d, n_in), e),
            _layer_spec((d, 128), e),
        ],
        out_specs=[_const_spec((n, n_in)), _const_spec((n, 128))],
        out_shape=[jax.ShapeDtypeStruct((n, n_in), F32), jax.ShapeDtypeStruct((n, 128), F32)],
        compiler_params=pltpu.CompilerParams(
            dimension_semantics=("arbitrary",), vmem_limit_bytes=VMEM_LIMIT),
        name="even_proj",
    )(x, nw, win, wab)


def _rows_to_cols(r):
    n = r.shape[0]
    if n < 128:
        r = jnp.concatenate([r, jnp.zeros((128 - n, r.shape[1]), r.dtype)], axis=0)
    return r.T


def _even_step_body(p_ref, ab_ref, shg_ref, sgd_ref, sconv_ref, lb_ref, hgw_ref, convw_ref, alog_ref,
                    dtb_ref, gdw_ref, *rest, bb):
    o_ref, shg_o, sgd_o, sconv_o, oa_scr, ob_scr = rest[-6:]
    hd_w = HEAD_DIM
    mix = N_HEADS * hd_w
    n_qkv = 3 * mix
    q0 = 4 * mix
    ab = ab_ref[...]
    g_all = -jnp.exp(alog_ref[...]) * _softplus(ab + dtb_ref[...])
    eg_all = jnp.exp(g_all)
    beta_all = _sigmoid(ab)

    qkv = p_ref[:, q0:q0 + n_qkv]
    conv = qkv * convw_ref[GDN_CONV - 1:GDN_CONV, :]
    for j in range(GDN_CONV - 1):
        conv = conv + sconv_ref[:, j, :] * convw_ref[j:j + 1, :]
    for j in range(GDN_CONV - 2):
        sconv_o[:, j, :] = sconv_ref[:, j + 1, :]
    sconv_o[:, GDN_CONV - 2, :] = qkv
    conv = _silu(conv)

    for hd in range(N_HEADS):
        ls = slice(hd * hd_w, (hd + 1) * hd_w)
        q, k, logg = _hgrn_gates(p_ref[:, ls], p_ref[:, mix + hd * hd_w: mix + (hd + 1) * hd_w],
                                 lb_ref[:, ls])
        v = p_ref[:, 2 * mix + hd * hd_w: 2 * mix + (hd + 1) * hd_w]
        qc = _rows_to_cols(q)
        kc = _rows_to_cols(k)
        gc = _rows_to_cols(jnp.exp(logg))
        q2 = _l2n(conv[:, ls]) * hd_w ** -0.5
        k2 = _l2n(conv[:, mix + hd * hd_w: mix + (hd + 1) * hd_w])
        v2 = conv[:, 2 * mix + hd * hd_w: 2 * mix + (hd + 1) * hd_w]
        q2c = _rows_to_cols(q2)
        k2c = _rows_to_cols(k2)
        for j in range(bb):
            s = shg_ref[j, hd]
            s_new = s * gc[:, j:j + 1] + kc[:, j:j + 1] * v[j:j + 1, :]
            shg_o[j, hd] = s_new
            oa_scr[j:j + 1, ls] = jnp.sum(s_new * qc[:, j:j + 1], axis=0, keepdims=True)

            s2 = sgd_ref[j, hd] * eg_all[j:j + 1, hd:hd + 1]
            ks = jnp.sum(s2 * k2c[:, j:j + 1], axis=0, keepdims=True)
            vnew = beta_all[j:j + 1, N_HEADS + hd:N_HEADS + hd + 1] * (v2[j:j + 1, :] - ks)
            s2_new = s2 + k2c[:, j:j + 1] * vnew
            sgd_o[j, hd] = s2_new
            ob_scr[j:j + 1, ls] = jnp.sum(s2_new * q2c[:, j:j + 1], axis=0, keepdims=True)

    for hd in range(N_HEADS):
        ls = slice(hd * hd_w, (hd + 1) * hd_w)
        og = p_ref[:, 3 * mix + hd * hd_w: 3 * mix + (hd + 1) * hd_w]
        z = p_ref[:, q0 + n_qkv + hd * hd_w: q0 + n_qkv + (hd + 1) * hd_w]
        o_ref[:, ls] = (_head_rms(oa_scr[:, ls], hgw_ref[...]) * _sigmoid(og)).astype(BF16)
        o_ref[:, mix + hd * hd_w: mix + (hd + 1) * hd_w] = (
            _head_rms(ob_scr[:, ls], gdw_ref[...]) * _silu(z)).astype(BF16)


def _even_step(p, ab, shg, sgd, sconv, lb, hgw, convw, alog, dtb, gdw, e, prev, *, bb):
    n, n_in = p.shape
    mix = N_HEADS * HEAD_DIM
    n_qkv = 3 * mix
    d = 2 * mix
    body = functools.partial(_even_step_body, bb=bb)
    st_block = (None, bb, N_HEADS, HEAD_DIM, HEAD_DIM)
    st_spec = pl.BlockSpec(st_block, lambda i: (e, i, 0, 0, 0))
    cv_spec = pl.BlockSpec((None, bb, GDN_CONV - 1, n_qkv), lambda i: (e, i, 0, 0))
    n_fixed = 11
    prev = tuple(prev)
    return pl.pallas_call(
        body,
        grid=(n // bb,),
        in_specs=[
            pl.BlockSpec((bb, n_in), lambda i: (i, 0)),
            pl.BlockSpec((bb, 128), lambda i: (i, 0)),
            st_spec, st_spec, cv_spec,
            _layer_spec((1, mix), e),
            _layer_spec((1, HEAD_DIM), e),
            _layer_spec((GDN_CONV, n_qkv), e),
            _layer_spec((1, 128), e),
            _layer_spec((1, 128), e),
            _layer_spec((1, HEAD_DIM), e),
        ] + [pl.BlockSpec(memory_space=pl.ANY)] * len(prev),
        out_specs=[pl.BlockSpec((bb, d), lambda i: (i, 0)), st_spec, st_spec, cv_spec],
        out_shape=[
            jax.ShapeDtypeStruct((n, d), BF16),
            jax.ShapeDtypeStruct(shg.shape, F32),
            jax.ShapeDtypeStruct(sgd.shape, F32),
            jax.ShapeDtypeStruct(sconv.shape, F32),
        ],
        input_output_aliases={n_fixed + j: 1 + j for j in range(len(prev))},
        scratch_shapes=[pltpu.VMEM((bb, mix), F32), pltpu.VMEM((bb, mix), F32)],
        compiler_params=pltpu.CompilerParams(
            dimension_semantics=("arbitrary",), vmem_limit_bytes=VMEM_LIMIT),
        name="even_step",
    )(p, ab, shg, sgd, sconv, lb, hgw, convw, alog, dtb, gdw, *prev)


def _out_proj_body(o_ref, x_ref, w_ref, xo_ref):
    xo_ref[...] = x_ref[...] + _dot(o_ref[...], w_ref[...])


def _out_proj(o, x, wout, e):
    n, d = x.shape
    return pl.pallas_call(
        _out_proj_body,
        grid=(1,),
        in_specs=[_const_spec((n, d)), _const_spec((n, d)), _layer_spec((d, d), e)],
        out_specs=_const_spec((n, d)),
        out_shape=jax.ShapeDtypeStruct((n, d), F32),
        compiler_params=pltpu.CompilerParams(
            dimension_semantics=("arbitrary",), vmem_limit_bytes=VMEM_LIMIT),
        name="out_proj",
    )(o, x, wout)


def _pool_group_out(d_grp, g, wgrp_ref):
    return _dot(d_grp.astype(BF16), wgrp_ref[g])


def _pool_prompt_body(x_ref, nw_ref, win_ref, wgrp_ref, scale_ref, xo_ref, buf_ref, ext_scr, *, tb):
    t = pl.program_id(1)
    nt = pl.num_programs(1)
    carry = 16
    gd = ext_scr.shape[1] // len(POOL_WINDOWS)

    @pl.when(t == 0)
    def _init():
        ext_scr[0:carry, :] = jnp.zeros((carry, ext_scr.shape[1]), F32)

    x = x_ref[0]
    h = _rms(x, nw_ref[...]).astype(BF16)
    u = _dot(h, win_ref[...])
    ext_scr[carry:carry + tb, :] = u
    pos = t * tb + lax.broadcasted_iota(jnp.int32, (tb, 1), 0)
    for g, wlen in enumerate(POOL_WINDOWS):
        cs = slice(g * gd, (g + 1) * gd)
        s = ext_scr[:, cs]
        span = 1
        while span < wlen:
            s = s + pltpu.roll(s, span, 0)
            span *= 2
        cnt = jnp.minimum(pos + 1, wlen).astype(F32)
        dg = s[carry:carry + tb, :] / cnt - u[:, cs]
        y = _pool_group_out(dg, g, wgrp_ref) * scale_ref[:, cs]
        xo_ref[0, :, cs] = x[:, cs] + y
    keep = ext_scr[tb:tb + carry, :]
    ext_scr[0:carry, :] = keep

    @pl.when(t == nt - 1)
    def _fin():
        buf_ref[0] = ext_scr[carry - POOL_BUF:carry, :]


def _pool_prompt(x, nw, win, wgrp, scale, layer, o, *, tb):
    bsz, seq, d = x.shape
    gd = d // len(POOL_WINDOWS)
    tb = min(tb, seq)
    body = functools.partial(_pool_prompt_body, tb=tb)
    return pl.pallas_call(
        body,
        grid=(bsz, seq // tb),
        in_specs=[
            pl.BlockSpec((1, tb, d), lambda b, t: (b, t, 0)),
            _layer_spec((1, d), layer),
            _layer_spec((d, d), o),
            _layer_spec((len(POOL_WINDOWS), gd, gd), o),
            _layer_spec((1, d), o),
        ],
        out_specs=[
            pl.BlockSpec((1, tb, d), lambda b, t: (b, t, 0)),
            pl.BlockSpec((1, POOL_BUF, d), lambda b, t: (b, 0, 0)),
        ],
        out_shape=[
            jax.ShapeDtypeStruct((bsz, seq, d), F32),
            jax.ShapeDtypeStruct((bsz, POOL_BUF, d), F32),
        ],
        scratch_shapes=[pltpu.VMEM((tb + 16, d), F32)],
        compiler_params=pltpu.CompilerParams(
            dimension_semantics=("arbitrary", "arbitrary"), vmem_limit_bytes=VMEM_LIMIT),
        name="pool_prompt",
    )(x, nw, win, wgrp, scale)


def _pool_step_body(x_ref, buf_ref, nw_ref, win_ref, wgrp_ref, scale_ref, *rest):
    xo_ref, buf_o, u_scr, win_scr = rest[-4:]
    n, d = x_ref.shape
    gd = d // len(POOL_WINDOWS)
    x = x_ref[...]
    h = _rms(x, nw_ref[...]).astype(BF16)
    u = _dot(h, win_ref[...])
    u_scr[...] = u
    lane_grp = lax.broadcasted_iota(jnp.int32, (POOL_BUF, d), 1) // gd
    rowi = lax.broadcasted_iota(jnp.int32, (POOL_BUF, d), 0)
    wlen = jnp.zeros((POOL_BUF, d), jnp.int32)
    for g, wl in enumerate(POOL_WINDOWS):
        wlen = jnp.where(lane_grp == g, wl, wlen)
    in_win = rowi >= (POOL_BUF + 1 - wlen)

    def per_seq(b, carry):
        rows = buf_ref[b]
        win_scr[pl.ds(b, 1), :] = jnp.sum(jnp.where(in_win, rows, 0.0), axis=0, keepdims=True)
        buf_o[b, 0:POOL_BUF - 1, :] = buf_ref[b, 1:POOL_BUF, :]
        buf_o[b, POOL_BUF - 1:POOL_BUF, :] = u_scr[pl.ds(b, 1), :]
        return carry

    lax.fori_loop(0, n, per_seq, 0)
    wsum = win_scr[...] + u
    for g, wl in enumerate(POOL_WINDOWS):
        cs = slice(g * gd, (g + 1) * gd)
        dg = wsum[:, cs] / float(wl) - u[:, cs]
        y = _pool_group_out(dg, g, wgrp_ref) * scale_ref[:, cs]
        xo_ref[:, cs] = x[:, cs] + y


def _pool_step(x, buf, nw, win, wgrp, scale, layer, o, prev, *, pos0):
    assert pos0 + 1 >= max(POOL_WINDOWS)
    n, d = x.shape
    gd = d // len(POOL_WINDOWS)
    return pl.pallas_call(
        _pool_step_body,
        grid=(1,),
        in_specs=[
            _const_spec((n, d)),
            pl.BlockSpec((None, n, POOL_BUF, d), lambda i: (o, 0, 0, 0)),
            _layer_spec((1, d), layer),
            _layer_spec((d, d), o),
            _layer_spec((len(POOL_WINDOWS), gd, gd), o),
            _layer_spec((1, d), o),
        ] + [pl.BlockSpec(memory_space=pl.ANY)] * len(prev),
        out_specs=[_const_spec((n, d)),
                   pl.BlockSpec((None, n, POOL_BUF, d), lambda i: (o, 0, 0, 0))],
        out_shape=[jax.ShapeDtypeStruct((n, d), F32), jax.ShapeDtypeStruct(buf.shape, F32)],
        input_output_aliases={6 + j: 1 + j for j in range(len(prev))},
        scratch_shapes=[pltpu.VMEM((n, d), F32), pltpu.VMEM((n, d), F32)],
        compiler_params=pltpu.CompilerParams(
            dimension_semantics=("arbitrary",), vmem_limit_bytes=VMEM_LIMIT),
        name="pool_step",
    )(x, buf, nw, win, wgrp, scale, *prev)


def _pad_lanes(a, width=128):
    return jnp.pad(a, [(0, 0)] * (a.ndim - 1) + [(0, width - a.shape[-1])])


def kernel(x_prompt, x_sample, state_hgrn, state_gdn, state_gdn_conv, state_pool, hgrn_lb, w_in_even, hgrn_norm_w, gdn_conv_w, gdn_a_log, gdn_dt_bias, gdn_norm_w, w_out_even, pool_w_in, pool_w_group, pool_scale, norm_ffn1, ffn1_w_gate, ffn1_w_up, ffn1_w_down, norm_mix, norm_ffn2, ffn2_w_gate, ffn2_w_up, ffn2_w_down, final_norm):
    depth, d = norm_mix.shape
    bsz, seq, _ = x_prompt.shape
    n_s, dec_seq, _ = x_sample.shape
    assert dec_seq == 1
    mix = N_HEADS * HEAD_DIM
    n_main = 8 * mix
    past_len = 16384

    bf = lambda a: a.astype(BF16)
    win_main = bf(w_in_even[:, :, :n_main])
    win_ab = bf(_pad_lanes(w_in_even[:, :, n_main:]))
    wout = bf(w_out_even)
    pwin, pwgrp = bf(pool_w_in), bf(pool_w_group)
    f1 = (bf(ffn1_w_gate), bf(ffn1_w_up), bf(ffn1_w_down))
    f2 = (bf(ffn2_w_gate), bf(ffn2_w_up), bf(ffn2_w_down))
    lb_all = jnp.cumsum(jax.nn.softmax(hgrn_lb.astype(F32), axis=0), axis=0)
    lb_all = (lb_all - lb_all[:1])[:, None, :]
    row = lambda a: a[:, None, :]
    nf1, nmx, nf2 = row(norm_ffn1), row(norm_mix), row(norm_ffn2)
    hgw, gdw = row(hgrn_norm_w), row(gdn_norm_w)
    alog, dtb = row(_pad_lanes(gdn_a_log)), row(_pad_lanes(gdn_dt_bias))
    pscale = row(pool_scale)
    fnw = final_norm[None, :]

    xp = x_prompt
    xs = x_sample.reshape(n_s, d)
    hg_p, gd_p, cv_p, pl_p = [], [], [], []
    even_s, pool_s = (), ()
    for li in range(depth):
        last = li == depth - 1
        pre_s = None
        xp = _ffn(xp.reshape(bsz * seq, d), nf1, *f1, fnw, li, final=False, tm=512).reshape(bsz, seq, d)
        xs = _ffn(xs, nf1, *f1, fnw, li, final=False, tm=512)
        if li % 2 == 0:
            e = li // 2
            xp, a, b, c = _even_prompt(xp, nmx, win_main, win_ab, lb_all, hgw, gdn_conv_w, alog, dtb,
                                       gdw, wout, li, e, tb=256)
            hg_p.append(a), gd_p.append(b), cv_p.append(c)
            p, ab = _even_proj(xs, nmx, win_main, win_ab, li, e)
            o, *even_s = _even_step(p, ab, state_hgrn, state_gdn, state_gdn_conv, lb_all, hgw,
                                    gdn_conv_w, alog, dtb, gdw, e, even_s, bb=8)
            pre_s = (o, wout, e)
        else:
            o = li // 2
            xp, pb = _pool_prompt(xp, nmx, pwin, pwgrp, pscale, li, o, tb=512)
            pl_p.append(pb)
            xs, *pool_s = _pool_step(xs, state_pool, nmx, pwin, pwgrp, pscale, li, o, pool_s,
                                     pos0=past_len)
        xp = _ffn(xp.reshape(bsz * seq, d), nf2, *f2, fnw, li, final=last, tm=512).reshape(bsz, seq, d)
        xs = _ffn(xs, nf2, *f2, fnw, li, final=last, tm=512, pre=pre_s)

    st = jnp.stack
    return (xp, xs.reshape(n_s, 1, d), st(hg_p), st(gd_p), st(cv_p), st(pl_p), *even_s, *pool_s)
```
